```python
import jax, jax.numpy as jnp
from jax import lax
import numpy as np

D_MODEL = 1024
BATCH = 8
SEQ = 4096
DEPTH = 1

CTX_LEN = 256
GRID_W = 64
D_MIX = D_MODEL
D_GMLP = D_MIX // 2
G_A = 4
GA_DIM = D_GMLP // G_A
CHUNK_A = 2 * GRID_W
D_MLSTM = D_MIX - D_GMLP
H_B = 4
HV = D_MLSTM // H_B
HK = HV // 2
D_QK = H_B * HK
CONV_W = 3
CHUNK_B = 128
N_GATES = 4 * H_B
D_IN = 2 * D_GMLP + 2 * D_QK + 2 * D_MLSTM + N_GATES
D_FF = 4 * D_MODEL
ALPHA = (2 * DEPTH) ** 0.25
BETA = (8 * DEPTH) ** -0.25
LN_EPS = 1e-5

kernel_name = "hymba_style_gmlp_mlstm_dit_layer"


def _ln_stats(x):
    xf = x.astype(jnp.float32)
    mu = jnp.mean(xf, axis=-1, keepdims=True)
    var = jnp.mean(jnp.square(xf - mu), axis=-1, keepdims=True)
    return (xf - mu) * lax.rsqrt(var + LN_EPS)


def layer_norm(x, g, b):
    return (_ln_stats(x) * g + b).astype(x.dtype)


def modulate(x, shift, scale):
    return _ln_stats(x).astype(x.dtype) * (1 + scale) + shift


def ada_split(cond, w_ada, b_ada):
    return jnp.split(jax.nn.silu(cond) @ w_ada + b_ada, 6, axis=-1)


def split_proj(p):
    bounds = [D_GMLP, 2 * D_GMLP, 2 * D_GMLP + D_QK, 2 * D_GMLP + 2 * D_QK,
              2 * D_GMLP + 2 * D_QK + D_MLSTM, 2 * D_GMLP + 2 * D_QK + 2 * D_MLSTM]
    return jnp.split(p, bounds, axis=-1)


def spatial_gating(u, v, w_s, b_s, g_v, b_v):
    bsz, s, _ = v.shape
    vc = layer_norm(v, g_v, b_v).reshape(bsz, s // CHUNK_A, CHUNK_A, G_A, GA_DIM)
    mixed = jnp.einsum('gts,bnsgd->bntgd', w_s, vc) + b_s.T[:, :, None]
    return u * mixed.reshape(bsz, s, D_GMLP)


def short_conv(x, w):
    pad = CONV_W // 2
    return lax.conv_general_dilated(
        x, w[:, None, :], window_strides=(1,), padding=[(pad, pad)],
        dimension_numbers=('NWC', 'WIO', 'NWC'), feature_group_count=x.shape[-1])


def mlstm_inputs(q, k, vv, gates, conv_w, b_gates):
    bsz, s, _ = q.shape
    qk = jax.nn.silu(short_conv(jnp.concatenate([q, k], axis=-1), conv_w))
    q, k = jnp.split(qk, 2, axis=-1)
    heads = lambda a, d: a.reshape(bsz, s, H_B, d).transpose(0, 2, 1, 3)
    g = (gates + b_gates).reshape(bsz, s, 4, H_B).transpose(2, 0, 3, 1)
    return heads(q, HK), heads(k, HK), heads(vv, HV), g


def mlstm_chunkwise(q, k, v, ig, fg, state):
    bsz, nh, s, _ = q.shape
    nc = s // CHUNK_B
    f32 = jnp.float32
    q = q.astype(f32) * (HK ** -0.5)
    k = k.astype(f32)
    v = v.astype(f32)
    ig = ig.astype(f32)
    logf = jax.nn.log_sigmoid(fg.astype(f32))

    def to_chunks(a):
        return jnp.moveaxis(a.reshape(bsz, nh, nc, CHUNK_B, *a.shape[3:]), 2, 0)

    tri = jnp.tril(jnp.ones((CHUNK_B, CHUNK_B), dtype=bool))

    def step(carry, xs_c):
        c0, n0, m0 = carry
        qc, kc, vc, ic, lfc = xs_c
        b = jnp.cumsum(lfc, axis=-1)
        dmat = jnp.where(tri, b[..., :, None] - b[..., None, :] + ic[..., None, :], -jnp.inf)
        inter = b + m0[..., None]
        m = jnp.maximum(inter, jnp.max(dmat, axis=-1))
        w_intra = jnp.exp(dmat - m[..., None])
        w_inter = jnp.exp(inter - m)
        scores = jnp.einsum('bhtk,bhsk->bhts', qc, kc) * w_intra
        num = (jnp.einsum('bhts,bhsv->bhtv', scores, vc)
               + w_inter[..., None] * jnp.einsum('bhtk,bhkv->bhtv', qc, c0))
        den = jnp.sum(scores, axis=-1) + w_inter * jnp.einsum('bhtk,bhk->bht', qc, n0)
        h = num / jnp.maximum(jnp.abs(den), jnp.exp(-m))[..., None]
        b_last = b[..., -1]
        dec = b_last[..., None] - b + ic
        m_new = jnp.maximum(b_last + m0, jnp.max(dec, axis=-1))
        w_s = jnp.exp(dec - m_new[..., None])
        carry_w = jnp.exp(b_last + m0 - m_new)
        c_new = carry_w[..., None, None] * c0 + jnp.einsum('bhs,bhsk,bhsv->bhkv', w_s, kc, vc)
        n_new = carry_w[..., None] * n0 + jnp.einsum('bhs,bhsk->bhk', w_s, kc)
        return (c_new, n_new, m_new), h

    xs = (to_chunks(q), to_chunks(k), to_chunks(v), to_chunks(ig), to_chunks(logf))
    state, hs = lax.scan(step, state, xs)
    h = jnp.moveaxis(hs, 0, 2).reshape(bsz, nh, s, HV)
    return h, state


def mlstm_bidirectional(ctx_in, lat_in):
    qc, kc, vc, gc = ctx_in
    ql, kl, vl, gl = lat_in
    bsz = ql.shape[0]
    f32 = jnp.float32
    zero = (jnp.zeros((bsz, H_B, HK, HV), f32), jnp.zeros((bsz, H_B, HK), f32), jnp.zeros((bsz, H_B), f32))
    flip = lambda a: jnp.flip(a, axis=2)
    hc_f, st_f = mlstm_chunkwise(qc, kc, vc, gc[0], gc[1], zero)
    hl_f, _ = mlstm_chunkwise(ql, kl, vl, gl[0], gl[1], st_f)
    hc_b, st_b = mlstm_chunkwise(flip(qc), flip(kc), flip(vc), flip(gc[2]), flip(gc[3]), zero)
    hl_b, _ = mlstm_chunkwise(flip(ql), flip(kl), flip(vl), flip(gl[2]), flip(gl[3]), st_b)
    return hc_f + flip(hc_b), hl_f + flip(hl_b)


def head_norm(h, g):
    bsz, nh, s, dv = h.shape
    y = _ln_stats(h).transpose(0, 2, 1, 3).reshape(bsz, s, nh * dv)
    return y * g


def mixer_output(u, v, h_b, o, w_s, b_s, ln_v_g, ln_v_b, hn_g, w_out):
    y_a = spatial_gating(u, v, w_s, b_s, ln_v_g, ln_v_b)
    y_b = head_norm(h_b, hn_g).astype(o.dtype) * jax.nn.sigmoid(o)
    return jnp.concatenate([y_a, y_b], axis=-1) @ w_out


def channel_mlp(h, w1, b1, w2, b2):
    return jnp.square(jax.nn.relu(h @ w1 + b1)) @ w2 + b2


def deepnorm_update(x, y, gate, g, b):
    return layer_norm(ALPHA * x + gate * y, g, b)


def setup_inputs(seed: int = 0) -> dict:
    key = jax.random.key(seed)
    ks = jax.random.split(key, 28)
    nrm = lambda k, shape, s: jax.random.normal(k, shape, jnp.float32) * s
    L = DEPTH
    f_bias = jnp.linspace(3.0, 6.0, H_B, dtype=jnp.float32)
    b_gates = jnp.concatenate([
        nrm(ks[10], (L, H_B), 0.1), f_bias + nrm(ks[11], (L, H_B), 0.1),
        nrm(ks[12], (L, H_B), 0.1), f_bias + nrm(ks[13], (L, H_B), 0.1)], axis=-1)
    return {
        "x": nrm(ks[0], (BATCH, SEQ, D_MODEL), 1.0),
        "c": nrm(ks[1], (BATCH, D_MODEL), 1.0),
        "ctx": nrm(ks[2], (BATCH, CTX_LEN, D_MODEL), 1.0),
        "c_ctx": nrm(ks[3], (D_MODEL,), 1.0),
        "w_ada": nrm(ks[4], (L, D_MODEL, 6 * D_MODEL), D_MODEL ** -0.5),
        "b_ada": nrm(ks[5], (L, 6 * D_MODEL), 0.01),
        "w_in": nrm(ks[6], (L, D_MODEL, D_IN), D_MODEL ** -0.5),
        "w_s": nrm(ks[7], (L, G_A, CHUNK_A, CHUNK_A), CHUNK_A ** -0.5),
        "b_s": 1.0 + nrm(ks[8], (L, G_A, CHUNK_A), 0.01),
        "ln_v_g": 1.0 + nrm(ks[9], (L, D_GMLP), 0.01),
        "ln_v_b": nrm(ks[14], (L, D_GMLP), 0.01),
        "conv_qk": nrm(ks[15], (L, CONV_W, 2 * D_QK), CONV_W ** -0.5),
        "b_gates": b_gates,
        "hn_g": 1.0 + nrm(ks[16], (L, D_MLSTM), 0.01),
        "w_out": nrm(ks[17], (L, D_MIX, D_MODEL), BETA * D_MIX ** -0.5),
        "ln1_g": 1.0 + nrm(ks[18], (L, D_MODEL), 0.01),
        "ln1_b": nrm(ks[19], (L, D_MODEL), 0.01),
        "w1": nrm(ks[20], (L, D_MODEL, D_FF), D_MODEL ** -0.5),
        "b1": nrm(ks[21], (L, D_FF), 0.01),
        "w2": nrm(ks[22], (L, D_FF, D_MODEL), BETA * D_FF ** -0.5),
        "b2": nrm(ks[23], (L, D_MODEL), 0.01),
        "ln2_g": 1.0 + nrm(ks[24], (L, D_MODEL), 0.01),
        "ln2_b": nrm(ks[25], (L, D_MODEL), 0.01),
    }


def reference(x, c, ctx, c_ctx, w_ada, b_ada, w_in, w_s, b_s, ln_v_g, ln_v_b, conv_qk, b_gates, hn_g,
              w_out, ln1_g, ln1_b, w1, b1, w2, b2, ln2_g, ln2_b):
    xc = ctx
    for l in range(DEPTH):
        sh1, sc1, g1, sh2, sc2, g2 = [m[:, None, :] for m in ada_split(c, w_ada[l], b_ada[l])]
        csh1, csc1, cg1, csh2, csc2, cg2 = ada_split(c_ctx, w_ada[l], b_ada[l])

        u_l, v_l, q_l, k_l, vv_l, o_l, gt_l = split_proj(modulate(x, sh1, sc1) @ w_in[l])
        u_c, v_c, q_c, k_c, vv_c, o_c, gt_c = split_proj(modulate(xc, csh1, csc1) @ w_in[l])

        hb_c, hb_l = mlstm_bidirectional(
            mlstm_inputs(q_c, k_c, vv_c, gt_c, conv_qk[l], b_gates[l]),
            mlstm_inputs(q_l, k_l, vv_l, gt_l, conv_qk[l], b_gates[l]))

        y_l = mixer_output(u_l, v_l, hb_l, o_l, w_s[l], b_s[l], ln_v_g[l], ln_v_b[l], hn_g[l], w_out[l])
        x_new = deepnorm_update(x, y_l, g1, ln1_g[l], ln1_b[l])
        x_new = deepnorm_update(x_new, channel_mlp(modulate(x_new, sh2, sc2), w1[l], b1[l], w2[l], b2[l]),
                                g2, ln2_g[l], ln2_b[l])

        if l < DEPTH - 1:
            y_c = mixer_output(u_c, v_c, hb_c, o_c, w_s[l], b_s[l], ln_v_g[l], ln_v_b[l], hn_g[l], w_out[l])
            xc = deepnorm_update(xc, y_c, cg1, ln1_g[l], ln1_b[l])
            xc = deepnorm_update(xc, channel_mlp(modulate(xc, csh2, csc2), w1[l], b1[l], w2[l], b2[l]),
                                 cg2, ln2_g[l], ln2_b[l])
        x = x_new
    return x
```

```python
import functools

import jax
import jax.numpy as jnp
from jax import lax
from jax.experimental import pallas as pl
from jax.experimental.pallas import tpu as pltpu

F32 = jnp.float32
BF16 = jnp.bfloat16

D_MODEL = 1024
D_GMLP = 512
G_A = 4
CHUNK = 128
D_MLSTM = 512
H_B = 4
HV = 128
HK = 64
D_QK = 256
D_FF = 4096
DEPTH = 1
ALPHA = (2 * DEPTH) ** 0.25
LN_EPS = 1e-5

HALO = 16
GATE_ROWS = 8
N_OG = D_MLSTM + 128
VMEM_LIMIT_BYTES = 56 * 1024 * 1024


def _ln_stats(x):
    mu = jnp.mean(x, axis=-1, keepdims=True)
    xc = x - mu
    var = jnp.mean(xc * xc, axis=-1, keepdims=True)
    return xc * lax.rsqrt(var + LN_EPS)


def _sigmoid(x):
    return 1.0 / (1.0 + jnp.exp(-x))


def _const_spec(shape):
    nd = len(shape)
    return pl.BlockSpec(shape, lambda *_: (0,) * nd, pipeline_mode=pl.Buffered(1))


def _ada_kernel(c_ref, w_ref, b_ref, o_ref):
    c = c_ref[...]
    s = (c * _sigmoid(c)).astype(BF16)
    o_ref[...] = jnp.dot(s, w_ref[...].astype(BF16), preferred_element_type=F32) + b_ref[...]


def _ada(cond, w, b):
    m, d = cond.shape
    n = w.shape[1]
    bn = 1536
    return pl.pallas_call(
        _ada_kernel,
        grid=(n // bn,),
        in_specs=[pl.BlockSpec((m, d), lambda j: (0, 0)),
                  pl.BlockSpec((d, bn), lambda j: (0, j)),
                  pl.BlockSpec((1, bn), lambda j: (0, j))],
        out_specs=pl.BlockSpec((m, bn), lambda j: (0, j)),
        out_shape=jax.ShapeDtypeStruct((m, n), F32),
        name="ada",
    )(cond, w, b)


def _proj_kernel(x_ref, xp_ref, xn_ref, sh_ref, sc_ref, wuv_ref, wqk_ref, wvv_ref, wog_ref,
                 ws_ref, bs_ref, lnvg_ref, lnvb_ref, cw_ref, qs_ref, bg_ref,
                 ya_ref, qk_ref, vv_ref, o_ref, gr_ref, hext_ref):
    i = pl.program_id(1)
    nt = pl.num_programs(1)
    tm = x_ref.shape[0]
    sc1 = 1.0 + sc_ref[...]
    sh = sh_ref[...]

    def mod(xv):
        return _ln_stats(xv) * sc1 + sh

    keep_prev = jnp.where(i > 0, 1.0, 0.0)
    keep_next = jnp.where(i < nt - 1, 1.0, 0.0)
    hext_ref[0:HALO, :] = (mod(xp_ref[...]) * keep_prev).astype(BF16)
    hext_ref[HALO:HALO + tm, :] = mod(x_ref[...]).astype(BF16)
    hext_ref[HALO + tm:, :] = (mod(xn_ref[...]) * keep_next).astype(BF16)
    hm = hext_ref[HALO:HALO + tm, :]

    pqk = jnp.dot(hext_ref[...], wqk_ref[...], preferred_element_type=F32)
    cw = cw_ref[...]
    conv = (pltpu.roll(pqk, 1, 0) * cw[0:1, :] + pqk * cw[1:2, :]
            + pltpu.roll(pqk, tm + 2 * HALO - 1, 0) * cw[2:3, :])
    conv = conv[HALO:HALO + tm, :]
    qk_ref[...] = (conv * _sigmoid(conv) * qs_ref[...]).astype(BF16)

    puv = jnp.dot(hm, wuv_ref[...], preferred_element_type=F32)
    lnv = (_ln_stats(puv[:, D_GMLP:]) * lnvg_ref[...] + lnvb_ref[...]).astype(BF16)
    for c in range(tm // CHUNK):
        r0 = c * CHUNK
        for g in range(G_A):
            c0 = g * CHUNK
            mixed = jnp.dot(ws_ref[g], lnv[r0:r0 + CHUNK, c0:c0 + CHUNK],
                            preferred_element_type=F32) + bs_ref[g]
            ya_ref[r0:r0 + CHUNK, c0:c0 + CHUNK] = (puv[r0:r0 + CHUNK, c0:c0 + CHUNK] * mixed).astype(BF16)

    vv_ref[...] = jnp.dot(hm, wvv_ref[...], preferred_element_type=F32).astype(BF16)

    pog = jnp.dot(hm, wog_ref[...], preferred_element_type=F32)
    o_ref[...] = pog[:, :D_MLSTM].astype(BF16)
    gt = pog[:, D_MLSTM:].T[0:H_B * GATE_ROWS, :] + bg_ref[...]
    for h in range(H_B):
        gr_ref[h] = gt[h * GATE_ROWS:(h + 1) * GATE_ROWS, :]


def _proj(x, sh, sc, wts, tm):
    bsz, s, d = x.shape
    nt = s // tm
    hb = tm // HALO
    nhb = s // HALO
    (wuv, wqk, wvv, wog, ws, bs, lnvg, lnvb, cw, qs, bg) = wts
    bg_t = jnp.broadcast_to(bg, (H_B * GATE_ROWS, tm))
    tok = lambda n, dt: jax.ShapeDtypeStruct((bsz, s, n), dt)
    tok_spec = lambda n: pl.BlockSpec((None, tm, n), lambda b, i: (b, i, 0))
    vec_spec = pl.BlockSpec((None, 1, d), lambda b, i: (b, 0, 0))
    return pl.pallas_call(
        _proj_kernel,
        grid=(bsz, nt),
        in_specs=[
            pl.BlockSpec((None, tm, d), lambda b, i: (b, i, 0)),
            pl.BlockSpec((None, HALO, d), lambda b, i: (b, jnp.maximum(i * hb - 1, 0), 0)),
            pl.BlockSpec((None, HALO, d), lambda b, i: (b, jnp.minimum((i + 1) * hb, nhb - 1), 0)),
            vec_spec, vec_spec,
            _const_spec(wuv.shape), _const_spec(wqk.shape), _const_spec(wvv.shape), _const_spec(wog.shape),
            _const_spec(ws.shape), _const_spec(bs.shape), _const_spec(lnvg.shape), _const_spec(lnvb.shape),
            _const_spec(cw.shape), _const_spec(qs.shape), _const_spec(bg_t.shape),
        ],
        out_specs=[tok_spec(D_GMLP), tok_spec(2 * D_QK), tok_spec(D_MLSTM), tok_spec(D_MLSTM),
                   pl.BlockSpec((None, H_B, GATE_ROWS, tm), lambda b, i: (b, 0, 0, i))],
        out_shape=[tok(D_GMLP, BF16), tok(2 * D_QK, BF16), tok(D_MLSTM, BF16), tok(D_MLSTM, BF16),
                   jax.ShapeDtypeStruct((bsz, H_B, GATE_ROWS, s), F32)],
        scratch_shapes=[pltpu.VMEM((tm + 2 * HALO, d), BF16)],
        compiler_params=pltpu.CompilerParams(
            dimension_semantics=("parallel", "arbitrary"), vmem_limit_bytes=VMEM_LIMIT_BYTES),
        name="proj",
    )(x, x, x, sh, sc, wuv, wqk, wvv, wog, ws, bs, lnvg, lnvb, cw, qs, bg_t)


def _prefix(x, op, lane, fill):
    d = 1
    while d < CHUNK:
        x = op(x, jnp.where(lane >= d, pltpu.roll(x, d, 1), fill))
        d *= 2
    return x


def _suffix(x, op, lane, fill):
    d = 1
    while d < CHUNK:
        x = op(x, jnp.where(lane < CHUNK - d, pltpu.roll(x, CHUNK - d, 1), fill))
        d *= 2
    return x


def _mlstm_kernel(qk_ref, vv_ref, o_ref, gr_ref, qkc_ref, vvc_ref, grc_ref, hng_ref, yb_ref,
                  c_s, cm_s, bb_s, cmax_s, btot_s, dc_s, st_s, m0f_s, m0b_s):
    ncx = qkc_ref.shape[0] // CHUNK
    ncl = qk_ref.shape[0] // CHUNK
    lane = lax.broadcasted_iota(jnp.int32, (GATE_ROWS, CHUNK), 1)
    row = lax.broadcasted_iota(jnp.int32, (GATE_ROWS, CHUNK), 0)
    fwd_rows = row < 2
    ones_col = jnp.where(lax.broadcasted_iota(jnp.int32, (CHUNK, CHUNK), 1) == 0, 1.0, 0.0).astype(BF16)

    def gate_prep(g, j):
        lf = jnp.minimum(g, 0.0) - jnp.log(1.0 + jnp.exp(-jnp.abs(g)))
        b = jnp.where(fwd_rows, _prefix(lf, jnp.add, lane, 0.0), _suffix(lf, jnp.add, lane, 0.0))
        btot = jnp.min(b, axis=1, keepdims=True)
        b = pltpu.roll(b, GATE_ROWS - 1, 0)
        btot = pltpu.roll(jnp.broadcast_to(btot, (GATE_ROWS, CHUNK)), GATE_ROWS - 1, 0)
        c = g - b
        cm = jnp.where(fwd_rows, _prefix(c, jnp.maximum, lane, -jnp.inf),
                       _suffix(c, jnp.maximum, lane, -jnp.inf))
        cmax = jnp.broadcast_to(jnp.max(c, axis=1, keepdims=True), (GATE_ROWS, CHUNK))
        c_s[j] = c
        cm_s[j] = cm
        bb_s[j] = b
        cmax_s[j] = cmax
        btot_s[j] = btot
        return c, cmax

    def delta_state(x, v, c, cmax, j):
        w = jnp.exp(c - cmax)
        kt = x.astype(F32).T[HK:, :]
        lhs = jnp.concatenate([kt * w[0:1, :], kt * w[2:3, :]], axis=0).astype(BF16)
        vaug = jnp.concatenate([v, ones_col], axis=1)
        dc_s[j] = jnp.dot(lhs, vaug, preferred_element_type=F32)

    for jc in range(ncx):
        sl = slice(jc * CHUNK, (jc + 1) * CHUNK)
        c, cmax = gate_prep(grc_ref[:, sl], jc)
        delta_state(qkc_ref[sl, :], vvc_ref[sl, :], c, cmax, jc)

    def prep_body(jl, carry):
        t0 = pl.multiple_of(jl * CHUNK, CHUNK)
        c, cmax = gate_prep(gr_ref[:, pl.ds(t0, CHUNK)], jl + ncx)
        delta_state(qk_ref[pl.ds(t0, CHUNK), :], vv_ref[pl.ds(t0, CHUNK), :], c, cmax, jl + ncx)
        return carry

    lax.fori_loop(0, ncl, prep_body, 0)

    def scan_step(j, state, m, r, half):
        cmax = cmax_s[j]
        g = jnp.maximum(m, cmax)
        a = jnp.exp(m - g)[r:r + 1, :]
        bc = jnp.exp(cmax - g)[r:r + 1, :]
        a2 = jnp.concatenate([a, a], axis=1)
        b2 = jnp.concatenate([bc, bc], axis=1)
        dc = dc_s[j]
        new_state = a2 * state + b2 * dc[half * HK:(half + 1) * HK, :]
        return new_state, btot_s[j] + g

    zero_state = jnp.zeros((HK, 2 * CHUNK), F32)
    zero_m = jnp.zeros((GATE_ROWS, CHUNK), F32)

    state, m = zero_state, zero_m
    for jc in range(ncx):
        state, m = scan_step(jc, state, m, 0, 0)

    def fwd_body(jl, carry):
        state, m = carry
        st_s[jl, 0] = state.astype(BF16)
        m0f_s[jl] = m
        return scan_step(jl + ncx, state, m, 0, 0)

    lax.fori_loop(0, ncl, fwd_body, (state, m))

    state, m = zero_state, zero_m
    for jc in reversed(range(ncx)):
        state, m = scan_step(jc, state, m, 2, 1)

    def bwd_body(t, carry):
        state, m = carry
        jl = ncl - 1 - t
        st_s[jl, 1] = state.astype(BF16)
        m0b_s[jl] = m
        return scan_step(jl + ncx, state, m, 2, 1)

    lax.fori_loop(0, ncl, bwd_body, (state, m))

    ti = lax.broadcasted_iota(jnp.int32, (CHUNK, CHUNK), 0)
    si = lax.broadcasted_iota(jnp.int32, (CHUNK, CHUNK), 1)
    mask_f = si <= ti
    mask_b = si >= ti
    hng = hng_ref[...]
    zpad = jnp.zeros((CHUNK - GATE_ROWS, CHUNK), F32)

    def out_body(jl, carry):
        j = jl + ncx
        t0 = pl.multiple_of(jl * CHUNK, CHUNK)
        c = c_s[j]
        m0 = jnp.where(fwd_rows, m0f_s[jl], m0b_s[jl])
        g = jnp.maximum(m0, cm_s[j])
        wi = jnp.exp(m0 - g)
        fl = jnp.exp(-(bb_s[j] + g))
        q8 = jnp.where((row == 0) | (row == 2), g,
                       jnp.where((row == 1) | (row == 3), pltpu.roll(wi, 1, 0), pltpu.roll(fl, 4, 0)))
        qt = jnp.concatenate([q8, zpad], axis=0).T

        x = qk_ref[pl.ds(t0, CHUNK), :]
        q = x[:, :HK]
        k = x[:, HK:]
        s = lax.dot_general(q, k, (((1,), (1,)), ((), ())), preferred_element_type=F32)
        vaug = jnp.concatenate([vv_ref[pl.ds(t0, CHUNK), :], ones_col], axis=1)
        qf = q.astype(F32)

        def direction(rc, cg, cw, cf, idx, mask):
            w = jnp.where(mask, jnp.exp(c[rc:rc + 1, :] - qt[:, cg:cg + 1]), 0.0)
            p = (s * w).astype(BF16)
            qw = (qf * qt[:, cw:cw + 1]).astype(BF16)
            nd = (jnp.dot(p, vaug, preferred_element_type=F32)
                  + jnp.dot(qw, st_s[jl, idx], preferred_element_type=F32))
            den = jnp.maximum(jnp.abs(nd[:, CHUNK:CHUNK + 1]), qt[:, cf:cf + 1])
            return nd[:, :CHUNK] / den

        h = direction(0, 0, 1, 4, 0, mask_f) + direction(2, 2, 3, 6, 1, mask_b)
        y = _ln_stats(h) * hng
        gate = _sigmoid(o_ref[pl.ds(t0, CHUNK), :].astype(F32))
        yb_ref[pl.ds(t0, CHUNK), :] = (y * gate).astype(BF16)
        return carry

    lax.fori_loop(0, ncl, out_body, 0)


def _mlstm(qk, vv, o, gr, qkc, vvc, grc, hng):
    bsz, s, _ = qk.shape
    sc = qkc.shape[1]
    ncl = s // CHUNK
    nc = ncl + sc // CHUNK
    head = lambda n: pl.BlockSpec((None, n, CHUNK), lambda b, h: (b, 0, h))
    gates = lambda n: pl.BlockSpec((None, None, GATE_ROWS, n), lambda b, h: (b, h, 0, 0))
    row_tile = lambda n: pltpu.VMEM((n, GATE_ROWS, CHUNK), F32)
    return pl.pallas_call(
        _mlstm_kernel,
        grid=(bsz, H_B),
        in_specs=[head(s), head(s), head(s), gates(s), head(sc), head(sc), gates(sc),
                  pl.BlockSpec((1, CHUNK), lambda b, h: (0, h))],
        out_specs=head(s),
        out_shape=jax.ShapeDtypeStruct((bsz, s, D_MLSTM), BF16),
        scratch_shapes=[row_tile(nc), row_tile(nc), row_tile(nc), row_tile(nc), row_tile(nc),
                        pltpu.VMEM((nc, 2 * HK, 2 * CHUNK), F32),
                        pltpu.VMEM((ncl, 2, HK, 2 * CHUNK), BF16),
                        row_tile(ncl), row_tile(ncl)],
        compiler_params=pltpu.CompilerParams(
            dimension_semantics=("parallel", "arbitrary"), vmem_limit_bytes=VMEM_LIMIT_BYTES),
        name="mlstm",
    )(qk, vv, o, gr, qkc, vvc, grc, hng)


def _post_kernel(x_ref, ya_ref, yb_ref, g1_ref, sh2_ref, sc2_ref, g2_ref, wo_ref, l1g_ref, l1b_ref,
                 w1_ref, b1_ref, w2_ref, b2_ref, l2g_ref, l2b_ref, out_ref):
    y = (jnp.dot(ya_ref[...], wo_ref[0:D_GMLP, :], preferred_element_type=F32)
         + jnp.dot(yb_ref[...], wo_ref[D_GMLP:, :], preferred_element_type=F32))
    x1 = _ln_stats(ALPHA * x_ref[...] + g1_ref[...] * y) * l1g_ref[...] + l1b_ref[...]
    h2 = (_ln_stats(x1) * (1.0 + sc2_ref[...]) + sh2_ref[...]).astype(BF16)
    nf = 1024
    z = jnp.zeros(x1.shape, F32)
    for f in range(D_FF // nf):
        t = jnp.dot(h2, w1_ref[:, f * nf:(f + 1) * nf], preferred_element_type=F32) + b1_ref[:, f * nf:(f + 1) * nf]
        t = jnp.maximum(t, 0.0)
        z = z + jnp.dot((t * t).astype(BF16), w2_ref[f * nf:(f + 1) * nf, :], preferred_element_type=F32)
    z = z + b2_ref[...]
    out_ref[...] = _ln_stats(ALPHA * x1 + g2_ref[...] * z) * l2g_ref[...] + l2b_ref[...]


def _post(x, ya, yb, g1, sh2, sc2, g2, wts, tm):
    bsz, s, d = x.shape
    (wo, l1g, l1b, w1, b1, w2, b2, l2g, l2b) = wts
    vec_spec = pl.BlockSpec((None, 1, d), lambda b, i: (b, 0, 0))
    tok_spec = lambda n: pl.BlockSpec((None, tm, n), lambda b, i: (b, i, 0))
    return pl.pallas_call(
        _post_kernel,
        grid=(bsz, s // tm),
        in_specs=[tok_spec(d), tok_spec(D_GMLP), tok_spec(D_MLSTM), vec_spec, vec_spec, vec_spec, vec_spec]
                 + [_const_spec(w.shape) for w in wts],
        out_specs=tok_spec(d),
        out_shape=jax.ShapeDtypeStruct((bsz, s, d), F32),
        compiler_params=pltpu.CompilerParams(
            dimension_semantics=("parallel", "arbitrary"), vmem_limit_bytes=VMEM_LIMIT_BYTES),
        name="post",
    )(x, ya, yb, g1, sh2, sc2, g2, wo, l1g, l1b, w1, b1, w2, b2, l2g, l2b)


def _proj_weights(w_in, w_s, b_s, ln_v_g, ln_v_b, conv_qk, b_gates):
    q0 = 2 * D_GMLP
    k0 = q0 + D_QK
    v0 = k0 + D_QK
    o0 = v0 + D_MLSTM
    g0 = o0 + D_MLSTM
    qk_cols = jnp.concatenate([jnp.concatenate([jnp.arange(q0 + HK * h, q0 + HK * (h + 1)),
                                                jnp.arange(k0 + HK * h, k0 + HK * (h + 1))]) for h in range(H_B)])
    conv_cols = qk_cols - q0
    slot = jnp.arange(N_OG - D_MLSTM)
    gate_src = (slot % GATE_ROWS) * H_B + (slot // GATE_ROWS) % H_B
    gate_used = (slot < H_B * GATE_ROWS) & (slot % GATE_ROWS < 4)
    gate_src = jnp.where(gate_used, gate_src, 0)
    gate_w = jnp.where(gate_used[None, :], w_in[:, g0 + gate_src], 0.0)
    gate_b = jnp.where(gate_used, b_gates[gate_src], 0.0)[:H_B * GATE_ROWS]
    wuv = w_in[:, :q0].astype(BF16)
    wqk = w_in[:, qk_cols].astype(BF16)
    wvv = w_in[:, v0:o0].astype(BF16)
    wog = jnp.concatenate([w_in[:, o0:g0], gate_w], axis=1).astype(BF16)
    ws = w_s.astype(BF16)
    bs = jnp.broadcast_to(b_s[:, :, None], (G_A, CHUNK, CHUNK))
    cw = conv_qk[:, conv_cols]
    qs = jnp.tile(jnp.concatenate([jnp.full((HK,), HK ** -0.5, F32), jnp.ones((HK,), F32)]), H_B)[None, :]
    return (wuv, wqk, wvv, wog, ws, bs, ln_v_g[None, :], ln_v_b[None, :], cw, qs, gate_b[:, None])


def kernel(x, c, ctx, c_ctx, w_ada, b_ada, w_in, w_s, b_s, ln_v_g, ln_v_b, conv_qk, b_gates, hn_g,
           w_out, ln1_g, ln1_b, w1, b1, w2, b2, ln2_g, ln2_b):
    bsz = x.shape[0]
    l = 0
    cond = jnp.concatenate([c, c_ctx[None, :], jnp.zeros((16 - bsz - 1, D_MODEL), F32)], axis=0)
    mods = _ada(cond, w_ada[l], b_ada[l][None, :])
    lat = [mods[:bsz, None, i * D_MODEL:(i + 1) * D_MODEL] for i in range(6)]
    cx = [jnp.broadcast_to(mods[bsz, i * D_MODEL:(i + 1) * D_MODEL], (bsz, 1, D_MODEL)) for i in range(2)]
    sh1, sc1, g1, sh2, sc2, g2 = lat

    pw = _proj_weights(w_in[l], w_s[l], b_s[l], ln_v_g[l], ln_v_b[l], conv_qk[l], b_gates[l])
    ya, qk, vv, o, gr = _proj(x, sh1, sc1, pw, 512)
    _, qkc, vvc, _, grc = _proj(ctx, cx[0], cx[1], pw, ctx.shape[1])

    yb = _mlstm(qk, vv, o, gr, qkc, vvc, grc, hn_g[l][None, :])

    row = lambda v: v[None, :]
    post_w = (w_out[l].astype(BF16), row(ln1_g[l]), row(ln1_b[l]), w1[l].astype(BF16), row(b1[l]),
              w2[l].astype(BF16), row(b2[l]), row(ln2_g[l]), row(ln2_b[l]))
    return _post(x, ya, yb, g1, sh2, sc2, g2, post_w, 512)
```

```python
import jax
import jax.numpy as jnp
from jax import lax
from jax.experimental import pallas as pl
from jax.experimental.pallas import tpu as pltpu

F32 = jnp.float32
BF16 = jnp.bfloat16

D_MODEL = 1024
D_GMLP = 512
G_A = 4
CHUNK = 128
D_MLSTM = 512
H_B = 4
HV = 128
HK = 64
D_QK = 256
D_FF = 4096
DEPTH = 1
ALPHA = (2 * DEPTH) ** 0.25
LN_EPS = 1e-5

HALO = 16
SUB = 8
GATE_COLS = 128
N_OG = D_MLSTM + GATE_COLS
VMEM_LIMIT_BYTES = 56 * 1024 * 1024


def _ln_stats(x):
    mu = jnp.mean(x, axis=-1, keepdims=True)
    xc = x - mu
    var = jnp.mean(xc * xc, axis=-1, keepdims=True)
    return xc * lax.rsqrt(var + LN_EPS)


def _sigmoid(x):
    return 1.0 / (1.0 + jnp.exp(-x))


def _const_spec(shape):
    nd = len(shape)
    return pl.BlockSpec(shape, lambda *_: (0,) * nd, pipeline_mode=pl.Buffered(1))


def _ada_kernel(c_ref, w_ref, b_ref, o_ref):
    c = c_ref[...]
    s = (c * _sigmoid(c)).astype(BF16)
    o_ref[...] = jnp.dot(s, w_ref[...].astype(BF16), preferred_element_type=F32) + b_ref[...]


def _ada(cond, w, b):
    m, d = cond.shape
    n = w.shape[1]
    bn = 1536
    return pl.pallas_call(
        _ada_kernel,
        grid=(n // bn,),
        in_specs=[pl.BlockSpec((m, d), lambda j: (0, 0)),
                  pl.BlockSpec((d, bn), lambda j: (0, j)),
                  pl.BlockSpec((1, bn), lambda j: (0, j))],
        out_specs=pl.BlockSpec((m, bn), lambda j: (0, j)),
        out_shape=jax.ShapeDtypeStruct((m, n), F32),
        name="ada",
    )(cond, w, b)


def _proj_kernel(x_ref, xp_ref, xn_ref, sh_ref, sc_ref, wuv_ref, wqk_ref, wvv_ref, wog_ref,
                 ws_ref, bs_ref, lnvg_ref, lnvb_ref, cw_ref, qs_ref, bg_ref,
                 ya_ref, qk_ref, vv_ref, o_ref, gr_ref, hext_ref):
    i = pl.program_id(1)
    nt = pl.num_programs(1)
    tm = x_ref.shape[0]
    sc1 = 1.0 + sc_ref[...]
    sh = sh_ref[...]

    def mod(xv):
        return _ln_stats(xv) * sc1 + sh

    keep_prev = jnp.where(i > 0, 1.0, 0.0)
    keep_next = jnp.where(i < nt - 1, 1.0, 0.0)
    hext_ref[0:HALO, :] = (mod(xp_ref[...]) * keep_prev).astype(BF16)
    hext_ref[HALO:HALO + tm, :] = mod(x_ref[...]).astype(BF16)
    hext_ref[HALO + tm:, :] = (mod(xn_ref[...]) * keep_next).astype(BF16)
    hm = hext_ref[HALO:HALO + tm, :]

    pqk = jnp.dot(hext_ref[...], wqk_ref[...], preferred_element_type=F32)
    cw = cw_ref[...]
    conv = (pltpu.roll(pqk, 1, 0) * cw[0:1, :] + pqk * cw[1:2, :]
            + pltpu.roll(pqk, tm + 2 * HALO - 1, 0) * cw[2:3, :])
    conv = conv[HALO:HALO + tm, :]
    qk_ref[...] = (conv * _sigmoid(conv) * qs_ref[...]).astype(BF16)

    puv = jnp.dot(hm, wuv_ref[...], preferred_element_type=F32)
    lnv = (_ln_stats(puv[:, D_GMLP:]) * lnvg_ref[...] + lnvb_ref[...]).astype(BF16)
    for c in range(tm // CHUNK):
        r0 = c * CHUNK
        for g in range(G_A):
            c0 = g * CHUNK
            mixed = jnp.dot(ws_ref[g], lnv[r0:r0 + CHUNK, c0:c0 + CHUNK],
                            preferred_element_type=F32) + bs_ref[g]
            ya_ref[r0:r0 + CHUNK, c0:c0 + CHUNK] = (puv[r0:r0 + CHUNK, c0:c0 + CHUNK] * mixed).astype(BF16)

    vv_ref[...] = jnp.dot(hm, wvv_ref[...], preferred_element_type=F32).astype(BF16)

    pog = jnp.dot(hm, wog_ref[...], preferred_element_type=F32)
    o_ref[...] = pog[:, :D_MLSTM].astype(BF16)
    gt = pog[:, D_MLSTM:].T[0:H_B * 2 * SUB, :] + bg_ref[...]
    for h in range(H_B):
        for kind in range(2):
            r0 = (2 * h + kind) * SUB
            for c in range(tm // CHUNK):
                gr_ref[h, kind, c * SUB:(c + 1) * SUB, :] = gt[r0:r0 + SUB, c * CHUNK:(c + 1) * CHUNK]


def _proj(x, sh, sc, wts, tm):
    bsz, s, d = x.shape
    nt = s // tm
    hb = tm // HALO
    nhb = s // HALO
    (wuv, wqk, wvv, wog, ws, bs, lnvg, lnvb, cw, qs, bg) = wts
    bg_t = jnp.broadcast_to(bg, (bg.shape[0], tm))
    tok = lambda n, dt: jax.ShapeDtypeStruct((bsz, s, n), dt)
    tok_spec = lambda n: pl.BlockSpec((None, tm, n), lambda b, i: (b, i, 0))
    vec_spec = pl.BlockSpec((None, 1, d), lambda b, i: (b, 0, 0))
    return pl.pallas_call(
        _proj_kernel,
        grid=(bsz, nt),
        in_specs=[
            pl.BlockSpec((None, tm, d), lambda b, i: (b, i, 0)),
            pl.BlockSpec((None, HALO, d), lambda b, i: (b, jnp.maximum(i * hb - 1, 0), 0)),
            pl.BlockSpec((None, HALO, d), lambda b, i: (b, jnp.minimum((i + 1) * hb, nhb - 1), 0)),
            vec_spec, vec_spec,
            _const_spec(wuv.shape), _const_spec(wqk.shape), _const_spec(wvv.shape), _const_spec(wog.shape),
            _const_spec(ws.shape), _const_spec(bs.shape), _const_spec(lnvg.shape), _const_spec(lnvb.shape),
            _const_spec(cw.shape), _const_spec(qs.shape), _const_spec(bg_t.shape),
        ],
        out_specs=[tok_spec(D_GMLP), tok_spec(2 * D_QK), tok_spec(D_MLSTM), tok_spec(D_MLSTM),
                   pl.BlockSpec((None, H_B, 2, (tm // CHUNK) * SUB, CHUNK), lambda b, i: (b, 0, 0, i, 0))],
        out_shape=[tok(D_GMLP, BF16), tok(2 * D_QK, BF16), tok(D_MLSTM, BF16), tok(D_MLSTM, BF16),
                   jax.ShapeDtypeStruct((bsz, H_B, 2, (s // CHUNK) * SUB, CHUNK), F32)],
        scratch_shapes=[pltpu.VMEM((tm + 2 * HALO, d), BF16)],
        compiler_params=pltpu.CompilerParams(
            dimension_semantics=("parallel", "arbitrary"), vmem_limit_bytes=VMEM_LIMIT_BYTES),
        name="proj",
    )(x, x, x, sh, sc, wuv, wqk, wvv, wog, ws, bs, lnvg, lnvb, cw, qs, bg_t)


def _prefix(x, op, lane, fill):
    d = 1
    while d < CHUNK:
        x = op(x, jnp.where(lane >= d, pltpu.roll(x, d, 1), fill))
        d *= 2
    return x


def _suffix(x, op, lane, fill):
    d = 1
    while d < CHUNK:
        x = op(x, jnp.where(lane < CHUNK - d, pltpu.roll(x, CHUNK - d, 1), fill))
        d *= 2
    return x


def _mlstm_kernel(qk_ref, vv_ref, o_ref, gr_ref, qkc_ref, vvc_ref, grc_ref, hng_ref, yb_ref,
                  c_s, cm_s, bb_s, cmax_s, btot_s, dc_s, st_s, m0f_s, m0b_s):
    ncx = qkc_ref.shape[0] // CHUNK
    ncl = qk_ref.shape[0] // CHUNK
    nc = ncx + ncl
    ones_col = jnp.where(lax.broadcasted_iota(jnp.int32, (CHUNK, CHUNK), 1) == 0, 1.0, 0.0).astype(BF16)

    gi = jnp.concatenate([grc_ref[0], gr_ref[0]], axis=0)
    gf = jnp.concatenate([grc_ref[1], gr_ref[1]], axis=0)
    lane = lax.broadcasted_iota(jnp.int32, gi.shape, 1)
    is_fwd = (lax.broadcasted_iota(jnp.int32, gi.shape, 0) & (SUB - 1)) == 0
    lf = jnp.minimum(gf, 0.0) - jnp.log(1.0 + jnp.exp(-jnp.abs(gf)))
    b = jnp.where(is_fwd, _prefix(lf, jnp.add, lane, 0.0), _suffix(lf, jnp.add, lane, 0.0))
    c = gi - b
    cm = jnp.where(is_fwd, _prefix(c, jnp.maximum, lane, -jnp.inf), _suffix(c, jnp.maximum, lane, -jnp.inf))
    c_s[...] = c
    cm_s[...] = cm
    bb_s[...] = b
    cmax_s[...] = jnp.broadcast_to(jnp.max(c, axis=1, keepdims=True), c.shape)
    btot_s[...] = jnp.broadcast_to(jnp.min(b, axis=1, keepdims=True), c.shape)

    def tile(ref, j):
        return ref[pl.ds(pl.multiple_of(j * SUB, SUB), SUB), :]

    def delta_state(x, v, j):
        w = jnp.exp(tile(c_s, j) - tile(cmax_s, j))
        kt = x.astype(F32).T[HK:, :]
        lhs = jnp.concatenate([kt * w[0:1, :], kt * w[1:2, :]], axis=0).astype(BF16)
        vaug = jnp.concatenate([v, ones_col], axis=1)
        dc_s[j] = jnp.dot(lhs, vaug, preferred_element_type=F32)

    for jc in range(ncx):
        sl = slice(jc * CHUNK, (jc + 1) * CHUNK)
        delta_state(qkc_ref[sl, :], vvc_ref[sl, :], jc)

    def prep_body(jl, carry):
        t0 = pl.multiple_of(jl * CHUNK, CHUNK)
        delta_state(qk_ref[pl.ds(t0, CHUNK), :], vv_ref[pl.ds(t0, CHUNK), :], jl + ncx)
        return carry

    lax.fori_loop(0, ncl, prep_body, 0, unroll=2)

    def scan_step(j, state, m, r):
        cmax = tile(cmax_s, j)
        g = jnp.maximum(m, cmax)
        a = jnp.exp(m - g)[r:r + 1, :]
        bc = jnp.exp(cmax - g)[r:r + 1, :]
        a2 = jnp.concatenate([a, a], axis=1)
        b2 = jnp.concatenate([bc, bc], axis=1)
        dc = dc_s[j]
        new_state = a2 * state + b2 * dc[r * HK:(r + 1) * HK, :]
        return new_state, tile(btot_s, j) + g

    zero_state = jnp.zeros((HK, 2 * CHUNK), F32)
    zero_m = jnp.zeros((SUB, CHUNK), F32)

    state, m = zero_state, zero_m
    for jc in range(ncx):
        state, m = scan_step(jc, state, m, 0)

    def fwd_body(jl, carry):
        state, m = carry
        st_s[jl, 0] = state.astype(BF16)
        m0f_s[pl.ds(pl.multiple_of(jl * SUB, SUB), SUB), :] = m
        return scan_step(jl + ncx, state, m, 0)

    lax.fori_loop(0, ncl, fwd_body, (state, m))

    state, m = zero_state, zero_m
    for jc in reversed(range(ncx)):
        state, m = scan_step(jc, state, m, 1)

    def bwd_body(t, carry):
        state, m = carry
        jl = ncl - 1 - t
        st_s[jl, 1] = state.astype(BF16)
        m0b_s[pl.ds(pl.multiple_of(jl * SUB, SUB), SUB), :] = m
        return scan_step(jl + ncx, state, m, 1)

    lax.fori_loop(0, ncl, bwd_body, (state, m))

    ti = lax.broadcasted_iota(jnp.int32, (CHUNK, CHUNK), 0)
    si = lax.broadcasted_iota(jnp.int32, (CHUNK, CHUNK), 1)
    mask_f = si <= ti
    mask_b = si >= ti
    row = lax.broadcasted_iota(jnp.int32, (SUB, CHUNK), 0)
    hng = hng_ref[...]
    zpad = jnp.zeros((CHUNK - SUB, CHUNK), F32)

    def out_body(jl, carry):
        j = jl + ncx
        t0 = pl.multiple_of(jl * CHUNK, CHUNK)
        c = tile(c_s, j)
        m0 = jnp.where(row == 0, tile(m0f_s, jl), tile(m0b_s, jl))
        g = jnp.maximum(m0, tile(cm_s, j))
        wi = jnp.exp(m0 - g)
        fl = jnp.exp(-(tile(bb_s, j) + g))
        q8 = jnp.where(row < 2, g, jnp.where(row < 4, pltpu.roll(wi, 2, 0), pltpu.roll(fl, 4, 0)))
        qt = jnp.concatenate([q8, zpad], axis=0).T

        x = qk_ref[pl.ds(t0, CHUNK), :]
        q = x[:, :HK]
        k = x[:, HK:]
        s = lax.dot_general(q, k, (((1,), (1,)), ((), ())), preferred_element_type=F32)
        vaug = jnp.concatenate([vv_ref[pl.ds(t0, CHUNK), :], ones_col], axis=1)
        qf = q.astype(F32)

        def direction(r, mask):
            w = jnp.where(mask, jnp.exp(c[r:r + 1, :] - qt[:, r:r + 1]), 0.0)
            p = (s * w).astype(BF16)
            qw = (qf * qt[:, 2 + r:3 + r]).astype(BF16)
            nd = (jnp.dot(p, vaug, preferred_element_type=F32)
                  + jnp.dot(qw, st_s[jl, r], preferred_element_type=F32))
            den = jnp.maximum(jnp.abs(nd[:, CHUNK:CHUNK + 1]), qt[:, 4 + r:5 + r])
            return nd[:, :CHUNK] / den

        h = direction(0, mask_f) + direction(1, mask_b)
        y = _ln_stats(h) * hng
        gate = _sigmoid(o_ref[pl.ds(t0, CHUNK), :].astype(F32))
        yb_ref[pl.ds(t0, CHUNK), :] = (y * gate).astype(BF16)
        return carry

    lax.fori_loop(0, ncl, out_body, 0, unroll=2)


def _mlstm(qk, vv, o, gr, qkc, vvc, grc, hng):
    bsz, s, _ = qk.shape
    sc = qkc.shape[1]
    ncl = s // CHUNK
    ncx = sc // CHUNK
    nc = ncl + ncx
    head = lambda n: pl.BlockSpec((None, n, CHUNK), lambda b, h: (b, 0, h))
    gates = lambda n: pl.BlockSpec((None, None, 2, n * SUB, CHUNK), lambda b, h: (b, h, 0, 0, 0))
    row_tiles = lambda n: pltpu.VMEM((n * SUB, CHUNK), F32)
    return pl.pallas_call(
        _mlstm_kernel,
        grid=(bsz, H_B),
        in_specs=[head(s), head(s), head(s), gates(ncl), head(sc), head(sc), gates(ncx),
                  pl.BlockSpec((1, CHUNK), lambda b, h: (0, h))],
        out_specs=head(s),
        out_shape=jax.ShapeDtypeStruct((bsz, s, D_MLSTM), BF16),
        scratch_shapes=[row_tiles(nc), row_tiles(nc), row_tiles(nc), row_tiles(nc), row_tiles(nc),
                        pltpu.VMEM((nc, 2 * HK, 2 * CHUNK), F32),
                        pltpu.VMEM((ncl, 2, HK, 2 * CHUNK), BF16),
                        row_tiles(ncl), row_tiles(ncl)],
        compiler_params=pltpu.CompilerParams(
            dimension_semantics=("parallel", "arbitrary"), vmem_limit_bytes=VMEM_LIMIT_BYTES),
        name="mlstm",
    )(qk, vv, o, gr, qkc, vvc, grc, hng)


def _post_kernel(x_ref, ya_ref, yb_ref, g1_ref, sh2_ref, sc2_ref, g2_ref, wo_ref, l1g_ref, l1b_ref,
                 w1_ref, b1_ref, w2_ref, b2_ref, l2g_ref, l2b_ref, out_ref):
    y = (jnp.dot(ya_ref[...], wo_ref[0:D_GMLP, :], preferred_element_type=F32)
         + jnp.dot(yb_ref[...], wo_ref[D_GMLP:, :], preferred_element_type=F32))
    x1 = _ln_stats(ALPHA * x_ref[...] + g1_ref[...] * y) * l1g_ref[...] + l1b_ref[...]
    h2 = (_ln_stats(x1) * (1.0 + sc2_ref[...]) + sh2_ref[...]).astype(BF16)
    nf = 1024
    z = jnp.zeros(x1.shape, F32)
    for f in range(D_FF // nf):
        t = jnp.dot(h2, w1_ref[:, f * nf:(f + 1) * nf], preferred_element_type=F32) + b1_ref[:, f * nf:(f + 1) * nf]
        t = jnp.maximum(t, 0.0)
        z = z + jnp.dot((t * t).astype(BF16), w2_ref[f * nf:(f + 1) * nf, :], preferred_element_type=F32)
    z = z + b2_ref[...]
    out_ref[...] = _ln_stats(ALPHA * x1 + g2_ref[...] * z) * l2g_ref[...] + l2b_ref[...]


def _post(x, ya, yb, g1, sh2, sc2, g2, wts, tm):
    bsz, s, d = x.shape
    (wo, l1g, l1b, w1, b1, w2, b2, l2g, l2b) = wts
    vec_spec = pl.BlockSpec((None, 1, d), lambda b, i: (b, 0, 0))
    tok_spec = lambda n: pl.BlockSpec((None, tm, n), lambda b, i: (b, i, 0))
    return pl.pallas_call(
        _post_kernel,
        grid=(bsz, s // tm),
        in_specs=[tok_spec(d), tok_spec(D_GMLP), tok_spec(D_MLSTM), vec_spec, vec_spec, vec_spec, vec_spec]
                 + [_const_spec(w.shape) for w in wts],
        out_specs=tok_spec(d),
        out_shape=jax.ShapeDtypeStruct((bsz, s, d), F32),
        compiler_params=pltpu.CompilerParams(
            dimension_semantics=("parallel", "arbitrary"), vmem_limit_bytes=VMEM_LIMIT_BYTES),
        name="post",
    )(x, ya, yb, g1, sh2, sc2, g2, wo, l1g, l1b, w1, b1, w2, b2, l2g, l2b)


def _proj_weights(w_in, w_s, b_s, ln_v_g, ln_v_b, conv_qk, b_gates):
    q0 = 2 * D_GMLP
    k0 = q0 + D_QK
    v0 = k0 + D_QK
    o0 = v0 + D_MLSTM
    g0 = o0 + D_MLSTM
    qk_cols = jnp.concatenate([jnp.concatenate([jnp.arange(q0 + HK * h, q0 + HK * (h + 1)),
                                                jnp.arange(k0 + HK * h, k0 + HK * (h + 1))]) for h in range(H_B)])
    conv_cols = qk_cols - q0
    slot = jnp.arange(GATE_COLS)
    head, kind, d = slot // (2 * SUB), (slot // SUB) % 2, slot % SUB
    gate_used = (slot < H_B * 2 * SUB) & (d < 2)
    gate_src = jnp.where(gate_used, (2 * d + kind) * H_B + head, 0)
    gate_w = jnp.where(gate_used[None, :], w_in[:, g0 + gate_src], 0.0)
    gate_b = jnp.where(gate_used, b_gates[gate_src], 0.0)[:H_B * 2 * SUB]
    wuv = w_in[:, :q0].astype(BF16)
    wqk = w_in[:, qk_cols].astype(BF16)
    wvv = w_in[:, v0:o0].astype(BF16)
    wog = jnp.concatenate([w_in[:, o0:g0], gate_w], axis=1).astype(BF16)
    ws = w_s.astype(BF16)
    bs = jnp.broadcast_to(b_s[:, :, None], (G_A, CHUNK, CHUNK))
    cw = conv_qk[:, conv_cols]
    qs = jnp.tile(jnp.concatenate([jnp.full((HK,), HK ** -0.5, F32), jnp.ones((HK,), F32)]), H_B)[None, :]
    return (wuv, wqk, wvv, wog, ws, bs, ln_v_g[None, :], ln_v_b[None, :], cw, qs, gate_b[:, None])


def kernel(x, c, ctx, c_ctx, w_ada, b_ada, w_in, w_s, b_s, ln_v_g, ln_v_b, conv_qk, b_gates, hn_g,
           w_out, ln1_g, ln1_b, w1, b1, w2, b2, ln2_g, ln2_b):
    bsz = x.shape[0]
    l = 0
    cond = jnp.concatenate([c, c_ctx[None, :], jnp.zeros((16 - bsz - 1, D_MODEL), F32)], axis=0)
    mods = _ada(cond, w_ada[l], b_ada[l][None, :])
    lat = [mods[:bsz, None, i * D_MODEL:(i + 1) * D_MODEL] for i in range(6)]
    cx = [jnp.broadcast_to(mods[bsz, i * D_MODEL:(i + 1) * D_MODEL], (bsz, 1, D_MODEL)) for i in range(2)]
    sh1, sc1, g1, sh2, sc2, g2 = lat

    pw = _proj_weights(w_in[l], w_s[l], b_s[l], ln_v_g[l], ln_v_b[l], conv_qk[l], b_gates[l])
    ya, qk, vv, o, gr = _proj(x, sh1, sc1, pw, 512)
    _, qkc, vvc, _, grc = _proj(ctx, cx[0], cx[1], pw, ctx.shape[1])

    yb = _mlstm(qk, vv, o, gr, qkc, vvc, grc, hn_g[l][None, :])

    row = lambda v: v[None, :]
    post_w = (w_out[l].astype(BF16), row(ln1_g[l]), row(ln1_b[l]), w1[l].astype(BF16), row(b1[l]),
              w2[l].astype(BF16), row(b2[l]), row(ln2_g[l]), row(ln2_b[l]))
    return _post(x, ya, yb, g1, sh2, sc2, g2, post_w, 512)
```

```python
import jax
import jax.numpy as jnp
from jax import lax
from jax.experimental import pallas as pl
from jax.experimental.pallas import tpu as pltpu

F32 = jnp.float32
BF16 = jnp.bfloat16

D_MODEL = 1024
D_GMLP = 512
G_A = 4
CHUNK = 128
D_MLSTM = 512
H_B = 4
HV = 128
HK = 64
D_QK = 256
D_FF = 4096
DEPTH = 1
ALPHA = (2 * DEPTH) ** 0.25
LN_EPS = 1e-5

HALO = 16
SUB = 8
AUG = 16
HVA = HV + AUG
GATE_COLS = 128
N_OG = D_MLSTM + GATE_COLS
VMEM_LIMIT_BYTES = 56 * 1024 * 1024
PROJ_TILE = 512
POST_TILE = 1024
POST_SUB = 256
OUT_GROUP = 4

NT_DIMS = (((1,), (1,)), ((), ()))


def _ln_stats(x):
    mu = jnp.mean(x, axis=-1, keepdims=True)
    xc = x - mu
    var = jnp.mean(xc * xc, axis=-1, keepdims=True)
    return xc * lax.rsqrt(var + LN_EPS)


def _sigmoid(x):
    return 1.0 / (1.0 + jnp.exp(-x))


def _const_spec(shape):
    nd = len(shape)
    return pl.BlockSpec(shape, lambda *_: (0,) * nd, pipeline_mode=pl.Buffered(1))


def _ada_kernel(c_ref, w_ref, b_ref, o_ref):
    c = c_ref[...]
    s = (c * _sigmoid(c)).astype(BF16)
    o_ref[...] = jnp.dot(s, w_ref[...].astype(BF16), preferred_element_type=F32) + b_ref[...]


def _ada(cond, w, b):
    m, d = cond.shape
    n = w.shape[1]
    bn = 1536
    return pl.pallas_call(
        _ada_kernel,
        grid=(n // bn,),
        in_specs=[pl.BlockSpec((m, d), lambda j: (0, 0)),
                  pl.BlockSpec((d, bn), lambda j: (0, j)),
                  pl.BlockSpec((1, bn), lambda j: (0, j))],
        out_specs=pl.BlockSpec((m, bn), lambda j: (0, j)),
        out_shape=jax.ShapeDtypeStruct((m, n), F32),
        name="ada",
    )(cond, w, b)


def _proj_kernel(x_ref, xp_ref, xn_ref, sh_ref, sc_ref, wuv_ref, wqk_ref, wvt_ref, wog_ref,
                 ws_ref, bs_ref, lnvg_ref, lnvb_ref, cw_ref, qs_ref, bg_ref,
                 ya_ref, kk_ref, qt_ref, vt_ref, o_ref, gr_ref, hext_ref):
    i = pl.program_id(1)
    nt = pl.num_programs(1)
    tm = x_ref.shape[0]
    sc1 = 1.0 + sc_ref[...]
    sh = sh_ref[...]

    def mod(xv):
        return _ln_stats(xv) * sc1 + sh

    keep_prev = jnp.where(i > 0, 1.0, 0.0)
    keep_next = jnp.where(i < nt - 1, 1.0, 0.0)
    hext_ref[0:HALO, :] = (mod(xp_ref[...]) * keep_prev).astype(BF16)
    hext_ref[HALO:HALO + tm, :] = mod(x_ref[...]).astype(BF16)
    hext_ref[HALO + tm:, :] = (mod(xn_ref[...]) * keep_next).astype(BF16)
    hm = hext_ref[HALO:HALO + tm, :]

    pqk = jnp.dot(hext_ref[...], wqk_ref[...], preferred_element_type=F32)
    cw = cw_ref[...]
    conv = (pltpu.roll(pqk, 1, 0) * cw[0:1, :] + pqk * cw[1:2, :]
            + pltpu.roll(pqk, tm + 2 * HALO - 1, 0) * cw[2:3, :])
    conv = conv[HALO:HALO + tm, :]
    act = conv * _sigmoid(conv) * qs_ref[...]
    lane = lax.broadcasted_iota(jnp.int32, (tm, CHUNK), 1)
    for pair in range(H_B // 2):
        blk = act[:, D_QK + pair * CHUNK:D_QK + (pair + 1) * CHUNK]
        swp = pltpu.roll(blk, HK, 1)
        kk_ref[:, (2 * pair) * CHUNK:(2 * pair + 1) * CHUNK] = jnp.where(lane < HK, blk, swp).astype(BF16)
        kk_ref[:, (2 * pair + 1) * CHUNK:(2 * pair + 2) * CHUNK] = jnp.where(lane < HK, swp, blk).astype(BF16)
    for pair in range(H_B // 2):
        for c in range(tm // CHUNK):
            blk = act[c * CHUNK:(c + 1) * CHUNK, pair * CHUNK:(pair + 1) * CHUNK]
            qt_ref[pair * CHUNK:(pair + 1) * CHUNK, c * CHUNK:(c + 1) * CHUNK] = blk.T.astype(BF16)

    puv = jnp.dot(hm, wuv_ref[...], preferred_element_type=F32)
    lnv = (_ln_stats(puv[:, D_GMLP:]) * lnvg_ref[...] + lnvb_ref[...]).astype(BF16)
    for c in range(tm // CHUNK):
        r0 = c * CHUNK
        for g in range(G_A):
            c0 = g * CHUNK
            mixed = jnp.dot(ws_ref[g], lnv[r0:r0 + CHUNK, c0:c0 + CHUNK],
                            preferred_element_type=F32) + bs_ref[g]
            ya_ref[r0:r0 + CHUNK, c0:c0 + CHUNK] = (puv[r0:r0 + CHUNK, c0:c0 + CHUNK] * mixed).astype(BF16)

    vt_ref[...] = lax.dot_general(wvt_ref[...], hm, NT_DIMS, preferred_element_type=F32).astype(BF16)

    pog = jnp.dot(hm, wog_ref[...], preferred_element_type=F32)
    o_ref[...] = pog[:, :D_MLSTM].astype(BF16)
    gt = pog[:, D_MLSTM:].T[0:H_B * 2 * SUB, :] + bg_ref[...]
    for h in range(H_B):
        for kind in range(2):
            r0 = (2 * h + kind) * SUB
            for c in range(tm // CHUNK):
                gr_ref[h, kind, c * SUB:(c + 1) * SUB, :] = gt[r0:r0 + SUB, c * CHUNK:(c + 1) * CHUNK]


def _proj(x, sh, sc, wts, tm):
    bsz, s, d = x.shape
    nt = s // tm
    hb = tm // HALO
    nhb = s // HALO
    (wuv, wqk, wvt, wog, ws, bs, lnvg, lnvb, cw, qs, bg) = wts
    bg_t = jnp.broadcast_to(bg, (bg.shape[0], tm))
    tok = lambda n: jax.ShapeDtypeStruct((bsz, s, n), BF16)
    tok_spec = lambda n: pl.BlockSpec((None, tm, n), lambda b, i: (b, i, 0))
    tr = lambda n: jax.ShapeDtypeStruct((bsz, n, s), BF16)
    tr_spec = lambda n: pl.BlockSpec((None, n, tm), lambda b, i: (b, 0, i))
    vec_spec = pl.BlockSpec((None, 1, d), lambda b, i: (b, 0, 0))
    return pl.pallas_call(
        _proj_kernel,
        grid=(bsz, nt),
        in_specs=[
            pl.BlockSpec((None, tm, d), lambda b, i: (b, i, 0)),
            pl.BlockSpec((None, HALO, d), lambda b, i: (b, jnp.maximum(i * hb - 1, 0), 0)),
            pl.BlockSpec((None, HALO, d), lambda b, i: (b, jnp.minimum((i + 1) * hb, nhb - 1), 0)),
            vec_spec, vec_spec,
            _const_spec(wuv.shape), _const_spec(wqk.shape), _const_spec(wvt.shape), _const_spec(wog.shape),
            _const_spec(ws.shape), _const_spec(bs.shape), _const_spec(lnvg.shape), _const_spec(lnvb.shape),
            _const_spec(cw.shape), _const_spec(qs.shape), _const_spec(bg_t.shape),
        ],
        out_specs=[tok_spec(D_GMLP), tok_spec(2 * D_QK), tr_spec(D_QK), tr_spec(D_MLSTM), tok_spec(D_MLSTM),
                   pl.BlockSpec((None, H_B, 2, (tm // CHUNK) * SUB, CHUNK), lambda b, i: (b, 0, 0, i, 0))],
        out_shape=[tok(D_GMLP), tok(2 * D_QK), tr(D_QK), tr(D_MLSTM), tok(D_MLSTM),
                   jax.ShapeDtypeStruct((bsz, H_B, 2, (s // CHUNK) * SUB, CHUNK), F32)],
        scratch_shapes=[pltpu.VMEM((tm + 2 * HALO, d), BF16)],
        compiler_params=pltpu.CompilerParams(
            dimension_semantics=("parallel", "arbitrary"), vmem_limit_bytes=VMEM_LIMIT_BYTES),
        name="proj",
    )(x, x, x, sh, sc, wuv, wqk, wvt, wog, ws, bs, lnvg, lnvb, cw, qs, bg_t)


def _prefix(x, op, lane, fill):
    d = 1
    while d < CHUNK:
        x = op(x, jnp.where(lane >= d, pltpu.roll(x, d, 1), fill))
        d *= 2
    return x


def _suffix(x, op, lane, fill):
    d = 1
    while d < CHUNK:
        x = op(x, jnp.where(lane < CHUNK - d, pltpu.roll(x, CHUNK - d, 1), fill))
        d *= 2
    return x


def _mlstm_kernel(kk_ref, qt_ref, vt_ref, o_ref, gr_ref, kkc_ref, vtc_ref, grc_ref, hng_ref, yb_ref,
                  c_s, cm_s, bb_s, cmax_s, btot_s, dc_s, stf_s, stb_s, m0f_s, m0b_s, rhs_s):
    ncx = kkc_ref.shape[0] // CHUNK
    ncl = kk_ref.shape[0] // CHUNK
    nc = ncx + ncl
    aug_rows = jnp.where(lax.broadcasted_iota(jnp.int32, (AUG, CHUNK), 0) == 0, 1.0, 0.0).astype(BF16)
    row = lax.broadcasted_iota(jnp.int32, (SUB, CHUNK), 0)
    lane1 = lax.broadcasted_iota(jnp.int32, (1, CHUNK), 1)
    lane_st = lax.broadcasted_iota(jnp.int32, (HVA, CHUNK), 1)

    gi = jnp.concatenate([grc_ref[0], gr_ref[0]], axis=0)
    gf = jnp.concatenate([grc_ref[1], gr_ref[1]], axis=0)
    lane = lax.broadcasted_iota(jnp.int32, gi.shape, 1)
    is_fwd = (lax.broadcasted_iota(jnp.int32, gi.shape, 0) & (SUB - 1)) == 0
    lf = jnp.minimum(gf, 0.0) - jnp.log(1.0 + jnp.exp(-jnp.abs(gf)))
    b = jnp.where(is_fwd, _prefix(lf, jnp.add, lane, 0.0), _suffix(lf, jnp.add, lane, 0.0))
    c = gi - b
    cm = jnp.where(is_fwd, _prefix(c, jnp.maximum, lane, -jnp.inf), _suffix(c, jnp.maximum, lane, -jnp.inf))
    c_s[...] = c
    cm_s[...] = cm
    bb_s[...] = b
    cmax_s[...] = jnp.broadcast_to(jnp.max(c, axis=1, keepdims=True), c.shape)
    btot_s[...] = jnp.broadcast_to(jnp.min(b, axis=1, keepdims=True), c.shape)

    def tile(ref, j):
        return ref[pl.ds(pl.multiple_of(j * SUB, SUB), SUB), :]

    def delta_state(kk, vt, j):
        w = jnp.exp(tile(c_s, j) - tile(cmax_s, j))
        vta = jnp.concatenate([vt, aug_rows], axis=0).astype(F32)
        lhs = jnp.concatenate([vta * w[0:1, :], vta * w[1:2, :]], axis=0).astype(BF16)
        d2 = jnp.dot(lhs, kk, preferred_element_type=F32)
        dc_s[j] = jnp.where(lane_st < HK, d2[:HVA, :], d2[HVA:, :])

    for jc in range(ncx):
        sl = slice(jc * CHUNK, (jc + 1) * CHUNK)
        delta_state(kkc_ref[sl, :], vtc_ref[:, sl], jc)

    def prep_body(jl, carry):
        t0 = pl.multiple_of(jl * CHUNK, CHUNK)
        delta_state(kk_ref[pl.ds(t0, CHUNK), :], vt_ref[:, pl.ds(t0, CHUNK)], jl + ncx)
        return carry

    lax.fori_loop(0, ncl, prep_body, 0, unroll=2)

    def scan_step(jf, jb, state, m):
        stf_s[jf] = state.astype(BF16)
        stb_s[jb] = state.astype(BF16)
        m0f_s[pl.ds(pl.multiple_of(jf * SUB, SUB), SUB), :] = m
        m0b_s[pl.ds(pl.multiple_of(jb * SUB, SUB), SUB), :] = m
        cmax = jnp.where(row == 0, tile(cmax_s, jf), tile(cmax_s, jb))
        btot = jnp.where(row == 0, tile(btot_s, jf), tile(btot_s, jb))
        g = jnp.maximum(m, cmax)
        a = jnp.exp(m - g)
        bc = jnp.exp(cmax - g)
        a_l = jnp.where(lane1 < HK, a[0:1, :], a[1:2, :])
        b_l = jnp.where(lane1 < HK, bc[0:1, :], bc[1:2, :])
        dc = jnp.where(lane_st < HK, dc_s[jf], dc_s[jb])
        return a_l * state + b_l * dc, btot + g

    state = jnp.zeros((HVA, CHUNK), F32)
    m = jnp.zeros((SUB, CHUNK), F32)
    for p in range(ncx):
        state, m = scan_step(p, ncx - 1 - p, state, m)

    def scan_body(p, carry):
        return scan_step(ncx + p, nc - 1 - p, *carry)

    lax.fori_loop(0, ncl, scan_body, (state, m))

    si = lax.broadcasted_iota(jnp.int32, (CHUNK, CHUNK), 0)
    ti = lax.broadcasted_iota(jnp.int32, (CHUNK, CHUNK), 1)
    masks = (si <= ti, si >= ti)
    hng = hng_ref[...]
    zpad = jnp.zeros((CHUNK - SUB, CHUNK), F32)
    zq = jnp.zeros((HK, CHUNK), BF16)

    def stabiliser(j):
        m0 = jnp.where(row == 0, tile(m0f_s, j), tile(m0b_s, j))
        return m0, jnp.maximum(m0, tile(cm_s, j))

    def weights_stage(jl):
        j = jl + ncx
        t0 = pl.multiple_of(jl * CHUNK, CHUNK)
        m0, g = stabiliser(j)
        wi = jnp.exp(m0 - g)
        ct = jnp.concatenate([tile(c_s, j), zpad], axis=0).T
        qt = qt_ref[:, pl.ds(t0, CHUNK)]
        st = lax.dot_general(kk_ref[pl.ds(t0, CHUNK), :], jnp.concatenate([qt, zq], axis=0),
                             (((1,), (0,)), ((), ())), preferred_element_type=F32)
        qtf = qt.astype(F32)
        for r in range(2):
            w = jnp.where(masks[r], jnp.exp(ct[:, r:r + 1] - g[r:r + 1, :]), 0.0)
            rhs_s[jl, 0:CHUNK, r * CHUNK:(r + 1) * CHUNK] = (st * w).astype(BF16)
            qw = (qtf * wi[r:r + 1, :]).astype(BF16)
            rhs_s[jl, CHUNK + r * HK:CHUNK + (r + 1) * HK, r * CHUNK:(r + 1) * CHUNK] = qw
            rhs_s[jl, CHUNK + (1 - r) * HK:CHUNK + (2 - r) * HK, r * CHUNK:(r + 1) * CHUNK] = zq

    def output_stage(jl):
        j = jl + ncx
        t0 = pl.multiple_of(jl * CHUNK, CHUNK)
        _, g = stabiliser(j)
        fl = jnp.exp(-(tile(bb_s, j) + g))
        vta = jnp.concatenate([vt_ref[:, pl.ds(t0, CHUNK)], aug_rows], axis=0)
        stt = jnp.where(lane_st < HK, stf_s[j], stb_s[j])
        nd = jnp.dot(jnp.concatenate([vta, stt], axis=1), rhs_s[jl], preferred_element_type=F32)
        ht = None
        for r in range(2):
            blk = nd[:, r * CHUNK:(r + 1) * CHUNK]
            rden = 1.0 / jnp.maximum(jnp.abs(blk[HV:HV + 1, :]), fl[r:r + 1, :])
            part = blk[:HV, :] * rden
            ht = part if ht is None else ht + part
        mu = jnp.mean(ht, axis=0, keepdims=True)
        xc = ht - mu
        var = jnp.mean(xc * xc, axis=0, keepdims=True)
        y = (xc * lax.rsqrt(var + LN_EPS)).T
        gate = _sigmoid(o_ref[pl.ds(t0, CHUNK), :].astype(F32))
        yb_ref[pl.ds(t0, CHUNK), :] = (y * hng * gate).astype(BF16)

    for u in range(OUT_GROUP):
        weights_stage(u)

    def out_body(i, carry):
        for u in range(OUT_GROUP):
            output_stage(i * OUT_GROUP + u)
        for u in range(OUT_GROUP):
            weights_stage((i + 1) * OUT_GROUP + u)
        return carry

    lax.fori_loop(0, ncl // OUT_GROUP - 1, out_body, 0)
    for u in range(OUT_GROUP):
        output_stage(ncl - OUT_GROUP + u)


def _mlstm(kk, qt, vt, o, gr, kkc, vtc, grc, hng):
    bsz, s, _ = kk.shape
    sc = kkc.shape[1]
    ncl = s // CHUNK
    ncx = sc // CHUNK
    nc = ncl + ncx
    head = lambda n: pl.BlockSpec((None, n, CHUNK), lambda b, h: (b, 0, h))
    head_t = lambda rows, n: pl.BlockSpec((None, rows, n), lambda b, h: (b, h, 0))
    gates = lambda n: pl.BlockSpec((None, None, 2, n * SUB, CHUNK), lambda b, h: (b, h, 0, 0, 0))
    row_tiles = lambda n: pltpu.VMEM((n * SUB, CHUNK), F32)
    return pl.pallas_call(
        _mlstm_kernel,
        grid=(bsz, H_B),
        in_specs=[head(s), head_t(HK, s), head_t(HV, s), head(s), gates(ncl),
                  head(sc), head_t(HV, sc), gates(ncx),
                  pl.BlockSpec((1, CHUNK), lambda b, h: (0, h))],
        out_specs=head(s),
        out_shape=jax.ShapeDtypeStruct((bsz, s, D_MLSTM), BF16),
        scratch_shapes=[row_tiles(nc), row_tiles(nc), row_tiles(nc), row_tiles(nc), row_tiles(nc),
                        pltpu.VMEM((nc, HVA, CHUNK), F32),
                        pltpu.VMEM((nc, HVA, CHUNK), BF16), pltpu.VMEM((nc, HVA, CHUNK), BF16),
                        row_tiles(nc), row_tiles(nc),
                        pltpu.VMEM((ncl, 2 * CHUNK, 2 * CHUNK), BF16)],
        compiler_params=pltpu.CompilerParams(
            dimension_semantics=("parallel", "arbitrary"), vmem_limit_bytes=VMEM_LIMIT_BYTES),
        name="mlstm",
    )(kk, qt, vt, o, gr, kkc, vtc, grc, hng)


def _post_kernel(x_ref, ya_ref, yb_ref, g1_ref, sh2_ref, sc2_ref, g2_ref, wo_ref, l1g_ref, l1b_ref,
                 w1_ref, b1_ref, w2_ref, b2_ref, l2g_ref, l2b_ref, out_ref):
    nf = 1024
    nparts = D_FF // nf
    nsub = x_ref.shape[0] // POST_SUB

    def head(r):
        rows = slice(r * POST_SUB, (r + 1) * POST_SUB)
        yab = jnp.concatenate([ya_ref[rows, :], yb_ref[rows, :]], axis=1)
        y = jnp.dot(yab, wo_ref[...], preferred_element_type=F32)
        x1 = _ln_stats(ALPHA * x_ref[rows, :] + g1_ref[...] * y) * l1g_ref[...] + l1b_ref[...]
        h2 = (_ln_stats(x1) * (1.0 + sc2_ref[...]) + sh2_ref[...]).astype(BF16)
        return x1, h2

    def mlp_part(h2, f):
        cols = slice(f * nf, (f + 1) * nf)
        t = jnp.maximum(jnp.dot(h2, w1_ref[:, cols], preferred_element_type=F32) + b1_ref[:, cols], 0.0)
        return jnp.dot((t * t).astype(BF16), w2_ref[cols, :], preferred_element_type=F32)

    def tail(r, x1, z):
        rows = slice(r * POST_SUB, (r + 1) * POST_SUB)
        out_ref[rows, :] = _ln_stats(ALPHA * x1 + g2_ref[...] * (z + b2_ref[...])) * l2g_ref[...] + l2b_ref[...]

    cur = head(0)
    pending = None
    for r in range(nsub):
        x1, h2 = cur
        z = mlp_part(h2, 0)
        if pending is not None:
            tail(*pending)
        z = z + mlp_part(h2, 1)
        if r + 1 < nsub:
            cur = head(r + 1)
        for f in range(2, nparts):
            z = z + mlp_part(h2, f)
        pending = (r, x1, z)
    tail(*pending)


def _post(x, ya, yb, g1, sh2, sc2, g2, wts, tm):
    bsz, s, d = x.shape
    (wo, l1g, l1b, w1, b1, w2, b2, l2g, l2b) = wts
    vec_spec = pl.BlockSpec((None, 1, d), lambda b, i: (b, 0, 0))
    tok_spec = lambda n: pl.BlockSpec((None, tm, n), lambda b, i: (b, i, 0))
    return pl.pallas_call(
        _post_kernel,
        grid=(bsz, s // tm),
        in_specs=[tok_spec(d), tok_spec(D_GMLP), tok_spec(D_MLSTM), vec_spec, vec_spec, vec_spec, vec_spec]
                 + [_const_spec(w.shape) for w in wts],
        out_specs=tok_spec(d),
        out_shape=jax.ShapeDtypeStruct((bsz, s, d), F32),
        compiler_params=pltpu.CompilerParams(
            dimension_semantics=("parallel", "arbitrary"), vmem_limit_bytes=VMEM_LIMIT_BYTES),
        name="post",
    )(x, ya, yb, g1, sh2, sc2, g2, wo, l1g, l1b, w1, b1, w2, b2, l2g, l2b)


def _proj_weights(w_in, w_s, b_s, ln_v_g, ln_v_b, conv_qk, b_gates):
    q0 = 2 * D_GMLP
    v0 = q0 + 2 * D_QK
    o0 = v0 + D_MLSTM
    g0 = o0 + D_MLSTM
    slot = jnp.arange(GATE_COLS)
    head, kind, d = slot // (2 * SUB), (slot // SUB) % 2, slot % SUB
    gate_used = (slot < H_B * 2 * SUB) & (d < 2)
    gate_src = jnp.where(gate_used, (2 * d + kind) * H_B + head, 0)
    gate_w = jnp.where(gate_used[None, :], w_in[:, g0 + gate_src], 0.0)
    gate_b = jnp.where(gate_used, b_gates[gate_src], 0.0)[:H_B * 2 * SUB]
    wuv = w_in[:, :q0].astype(BF16)
    wqk = w_in[:, q0:v0].astype(BF16)
    wvt = w_in[:, v0:o0].T.astype(BF16)
    wog = jnp.concatenate([w_in[:, o0:g0], gate_w], axis=1).astype(BF16)
    ws = w_s.astype(BF16)
    bs = jnp.broadcast_to(b_s[:, :, None], (G_A, CHUNK, CHUNK))
    qs = jnp.concatenate([jnp.full((D_QK,), HK ** -0.5, F32), jnp.ones((D_QK,), F32)])[None, :]
    return (wuv, wqk, wvt, wog, ws, bs, ln_v_g[None, :], ln_v_b[None, :], conv_qk, qs, gate_b[:, None])


def kernel(x, c, ctx, c_ctx, w_ada, b_ada, w_in, w_s, b_s, ln_v_g, ln_v_b, conv_qk, b_gates, hn_g,
           w_out, ln1_g, ln1_b, w1, b1, w2, b2, ln2_g, ln2_b):
    bsz = x.shape[0]
    l = 0
    cond = jnp.concatenate([c, c_ctx[None, :], jnp.zeros((16 - bsz - 1, D_MODEL), F32)], axis=0)
    mods = _ada(cond, w_ada[l], b_ada[l][None, :])
    lat = [mods[:bsz, None, i * D_MODEL:(i + 1) * D_MODEL] for i in range(6)]
    cx = [jnp.broadcast_to(mods[bsz, i * D_MODEL:(i + 1) * D_MODEL], (bsz, 1, D_MODEL)) for i in range(2)]
    sh1, sc1, g1, sh2, sc2, g2 = lat

    pw = _proj_weights(w_in[l], w_s[l], b_s[l], ln_v_g[l], ln_v_b[l], conv_qk[l], b_gates[l])
    ya, kk, qt, vt, o, gr = _proj(x, sh1, sc1, pw, PROJ_TILE)
    _, kkc, _, vtc, _, grc = _proj(ctx, cx[0], cx[1], pw, ctx.shape[1])

    yb = _mlstm(kk, qt, vt, o, gr, kkc, vtc, grc, hn_g[l][None, :])

    row = lambda v: v[None, :]
    post_w = (w_out[l].astype(BF16), row(ln1_g[l]), row(ln1_b[l]), w1[l].astype(BF16), row(b1[l]),
              w2[l].astype(BF16), row(b2[l]), row(ln2_g[l]), row(ln2_b[l]))
    return _post(x, ya, yb, g1, sh2, sc2, g2, post_w, POST_TILE)
```

```python
import jax
import jax.numpy as jnp
from jax import lax
from jax.experimental import pallas as pl
from jax.experimental.pallas import tpu as pltpu

F32 = jnp.float32
BF16 = jnp.bfloat16

D_MODEL = 1024
D_GMLP = 512
G_A = 4
CHUNK = 128
D_MLSTM = 512
H_B = 4
HV = 128
HK = 64
D_QK = 256
D_FF = 4096
DEPTH = 1
ALPHA = (2 * DEPTH) ** 0.25
LN_EPS = 1e-5

HALO = 16
SUB = 8
AUG = 16
HVA = HV + AUG
VMEM_LIMIT_BYTES = 56 * 1024 * 1024
PROJ_TILE = 1024
PROJ_SUB = 256
POST_TILE = 1024
POST_SUB = 256
OUT_GROUP = 4

NT_DIMS = (((1,), (1,)), ((), ()))


def _ln_stats(x):
    mu = jnp.mean(x, axis=-1, keepdims=True)
    xc = x - mu
    var = jnp.mean(xc * xc, axis=-1, keepdims=True)
    return xc * lax.rsqrt(var + LN_EPS)


def _sigmoid(x):
    return 1.0 / (1.0 + jnp.exp(-x))


def _const_spec(shape):
    nd = len(shape)
    return pl.BlockSpec(shape, lambda *_: (0,) * nd, pipeline_mode=pl.Buffered(1))


def _ada_kernel(c_ref, w_ref, b_ref, o_ref):
    c = c_ref[...]
    s = (c * _sigmoid(c)).astype(BF16)
    o_ref[...] = jnp.dot(s, w_ref[...].astype(BF16), preferred_element_type=F32) + b_ref[...]


def _ada(cond, w, b):
    m, d = cond.shape
    n = w.shape[1]
    bn = 1536
    return pl.pallas_call(
        _ada_kernel,
        grid=(n // bn,),
        in_specs=[pl.BlockSpec((m, d), lambda j: (0, 0)),
                  pl.BlockSpec((d, bn), lambda j: (0, j)),
                  pl.BlockSpec((1, bn), lambda j: (0, j))],
        out_specs=pl.BlockSpec((m, bn), lambda j: (0, j)),
        out_shape=jax.ShapeDtypeStruct((m, n), F32),
        name="ada",
    )(cond, w, b)


def _proj_kernel(x_ref, xp_ref, xn_ref, sh_ref, sc_ref, wuv_ref, wqk_ref, wvt_ref, wo_ref,
                 ws_ref, bs_ref, lnvg_ref, lnvb_ref, cw_ref, qs_ref, bg_ref,
                 ya_ref, kk_ref, qt_ref, vt_ref, o_ref, gr_ref):
    i = pl.program_id(1)
    nt = pl.num_programs(1)
    tm = x_ref.shape[0]
    ts = min(PROJ_SUB, tm)
    nsub = tm // ts
    sc1 = 1.0 + sc_ref[...]
    sh = sh_ref[...]
    cw = cw_ref[...]
    lane = lax.broadcasted_iota(jnp.int32, (ts, CHUNK), 1)

    def mod(xv):
        return _ln_stats(xv) * sc1 + sh

    def project(s, hm, hext):
        r0 = s * ts
        rows = slice(r0, r0 + ts)
        puv = jnp.dot(hm, wuv_ref[...], preferred_element_type=F32)
        pqk = jnp.dot(hext, wqk_ref[...], preferred_element_type=F32)
        vtg = lax.dot_general(wvt_ref[...], hm, NT_DIMS, preferred_element_type=F32)
        vt_ref[:, rows] = vtg[:D_MLSTM, :].astype(BF16)
        o_ref[rows, :] = jnp.dot(hm, wo_ref[...], preferred_element_type=F32).astype(BF16)

        conv = (pltpu.roll(pqk, 1, 0) * cw[0:1, :] + pqk * cw[1:2, :]
                + pltpu.roll(pqk, ts + 2 * HALO - 1, 0) * cw[2:3, :])
        conv = conv[HALO:HALO + ts, :]
        act = conv * _sigmoid(conv) * qs_ref[...]
        for pair in range(H_B // 2):
            blk = act[:, D_QK + pair * CHUNK:D_QK + (pair + 1) * CHUNK]
            swp = pltpu.roll(blk, HK, 1)
            kk_ref[rows, (2 * pair) * CHUNK:(2 * pair + 1) * CHUNK] = jnp.where(lane < HK, blk, swp).astype(BF16)
            kk_ref[rows, (2 * pair + 1) * CHUNK:(2 * pair + 2) * CHUNK] = jnp.where(lane < HK, swp, blk).astype(BF16)
            for c in range(ts // CHUNK):
                blk = act[c * CHUNK:(c + 1) * CHUNK, pair * CHUNK:(pair + 1) * CHUNK]
                qt_ref[pair * CHUNK:(pair + 1) * CHUNK, r0 + c * CHUNK:r0 + (c + 1) * CHUNK] = blk.T.astype(BF16)

        gt = vtg[D_MLSTM:, :] + bg_ref[...]
        for h in range(H_B):
            for kind in range(2):
                g0 = (2 * h + kind) * SUB
                for c in range(ts // CHUNK):
                    t0 = (r0 // CHUNK + c) * SUB
                    gr_ref[h, kind, t0:t0 + SUB, :] = gt[g0:g0 + SUB, c * CHUNK:(c + 1) * CHUNK]

        lnv = (_ln_stats(puv[:, D_GMLP:]) * lnvg_ref[...] + lnvb_ref[...]).astype(BF16)
        nch = ts // CHUNK
        for g in range(G_A):
            cols = slice(g * CHUNK, (g + 1) * CHUNK)
            side = jnp.concatenate([lnv[c * CHUNK:(c + 1) * CHUNK, cols] for c in range(nch)], axis=1)
            mixed = jnp.dot(ws_ref[g], side, preferred_element_type=F32)
            for c in range(nch):
                m_c = mixed[:, c * CHUNK:(c + 1) * CHUNK] + bs_ref[g]
                ya_ref[r0 + c * CHUNK:r0 + (c + 1) * CHUNK, cols] = (
                    puv[c * CHUNK:(c + 1) * CHUNK, cols] * m_c).astype(BF16)

    keep_prev = jnp.where(i > 0, 1.0, 0.0)
    keep_next = jnp.where(i < nt - 1, 1.0, 0.0)
    left = (mod(xp_ref[...]) * keep_prev).astype(BF16)
    cur = mod(x_ref[0:ts, :]).astype(BF16)
    for s in range(nsub):
        if s + 1 < nsub:
            nxt = mod(x_ref[(s + 1) * ts:(s + 2) * ts, :]).astype(BF16)
        else:
            nxt = (mod(xn_ref[...]) * keep_next).astype(BF16)
        project(s, cur, jnp.concatenate([left, cur, nxt[:HALO, :]], axis=0))
        left, cur = cur[ts - HALO:, :], nxt


def _proj(x, sh, sc, wts, tm):
    bsz, s, d = x.shape
    nt = s // tm
    hb = tm // HALO
    nhb = s // HALO
    (wuv, wqk, wvt, wo, ws, bs, lnvg, lnvb, cw, qs, bg) = wts
    bg_t = jnp.broadcast_to(bg, (bg.shape[0], min(PROJ_SUB, tm)))
    tok = lambda n: jax.ShapeDtypeStruct((bsz, s, n), BF16)
    tok_spec = lambda n: pl.BlockSpec((None, tm, n), lambda b, i: (b, i, 0))
    tr = lambda n: jax.ShapeDtypeStruct((bsz, n, s), BF16)
    tr_spec = lambda n: pl.BlockSpec((None, n, tm), lambda b, i: (b, 0, i))
    vec_spec = pl.BlockSpec((None, 1, d), lambda b, i: (b, 0, 0))
    return pl.pallas_call(
        _proj_kernel,
        grid=(bsz, nt),
        in_specs=[
            pl.BlockSpec((None, tm, d), lambda b, i: (b, i, 0)),
            pl.BlockSpec((None, HALO, d), lambda b, i: (b, jnp.maximum(i * hb - 1, 0), 0)),
            pl.BlockSpec((None, HALO, d), lambda b, i: (b, jnp.minimum((i + 1) * hb, nhb - 1), 0)),
            vec_spec, vec_spec,
            _const_spec(wuv.shape), _const_spec(wqk.shape), _const_spec(wvt.shape), _const_spec(wo.shape),
            _const_spec(ws.shape), _const_spec(bs.shape), _const_spec(lnvg.shape), _const_spec(lnvb.shape),
            _const_spec(cw.shape), _const_spec(qs.shape), _const_spec(bg_t.shape),
        ],
        out_specs=[tok_spec(D_GMLP), tok_spec(2 * D_QK), tr_spec(D_QK), tr_spec(D_MLSTM), tok_spec(D_MLSTM),
                   pl.BlockSpec((None, H_B, 2, (tm // CHUNK) * SUB, CHUNK), lambda b, i: (b, 0, 0, i, 0))],
        out_shape=[tok(D_GMLP), tok(2 * D_QK), tr(D_QK), tr(D_MLSTM), tok(D_MLSTM),
                   jax.ShapeDtypeStruct((bsz, H_B, 2, (s // CHUNK) * SUB, CHUNK), F32)],
        compiler_params=pltpu.CompilerParams(
            dimension_semantics=("parallel", "arbitrary"), vmem_limit_bytes=VMEM_LIMIT_BYTES),
        name="proj",
    )(x, x, x, sh, sc, wuv, wqk, wvt, wo, ws, bs, lnvg, lnvb, cw, qs, bg_t)


def _prefix(x, op, lane, fill):
    d = 1
    while d < CHUNK:
        x = op(x, jnp.where(lane >= d, pltpu.roll(x, d, 1), fill))
        d *= 2
    return x


def _suffix(x, op, lane, fill):
    d = 1
    while d < CHUNK:
        x = op(x, jnp.where(lane < CHUNK - d, pltpu.roll(x, CHUNK - d, 1), fill))
        d *= 2
    return x


def _mlstm_kernel(kk_ref, qt_ref, vt_ref, o_ref, gr_ref, kkc_ref, vtc_ref, grc_ref, hng_ref, yb_ref,
                  c_s, cm_s, bb_s, cmax_s, btot_s, dc_s, stf_s, stb_s, m0f_s, m0b_s, rhs_s):
    ncx = kkc_ref.shape[0] // CHUNK
    ncl = kk_ref.shape[0] // CHUNK
    nc = ncx + ncl
    aug_rows = jnp.where(lax.broadcasted_iota(jnp.int32, (AUG, CHUNK), 0) == 0, 1.0, 0.0).astype(BF16)
    row = lax.broadcasted_iota(jnp.int32, (SUB, CHUNK), 0)
    lane1 = lax.broadcasted_iota(jnp.int32, (1, CHUNK), 1)
    lane_st = lax.broadcasted_iota(jnp.int32, (HVA, CHUNK), 1)

    gi = jnp.concatenate([grc_ref[0], gr_ref[0]], axis=0)
    gf = jnp.concatenate([grc_ref[1], gr_ref[1]], axis=0)
    lane = lax.broadcasted_iota(jnp.int32, gi.shape, 1)
    is_fwd = (lax.broadcasted_iota(jnp.int32, gi.shape, 0) & (SUB - 1)) == 0
    lf = jnp.minimum(gf, 0.0) - jnp.log(1.0 + jnp.exp(-jnp.abs(gf)))
    b = jnp.where(is_fwd, _prefix(lf, jnp.add, lane, 0.0), _suffix(lf, jnp.add, lane, 0.0))
    c = gi - b
    cm = jnp.where(is_fwd, _prefix(c, jnp.maximum, lane, -jnp.inf), _suffix(c, jnp.maximum, lane, -jnp.inf))
    c_s[...] = c
    cm_s[...] = cm
    bb_s[...] = b
    cmax_s[...] = jnp.broadcast_to(jnp.max(c, axis=1, keepdims=True), c.shape)
    btot_s[...] = jnp.broadcast_to(jnp.min(b, axis=1, keepdims=True), c.shape)

    def tile(ref, j):
        return ref[pl.ds(pl.multiple_of(j * SUB, SUB), SUB), :]

    def delta_state(kk, vt, j):
        w = jnp.exp(tile(c_s, j) - tile(cmax_s, j))
        vta = jnp.concatenate([vt, aug_rows], axis=0).astype(F32)
        lhs = jnp.concatenate([vta * w[0:1, :], vta * w[1:2, :]], axis=0).astype(BF16)
        d2 = jnp.dot(lhs, kk, preferred_element_type=F32)
        dc_s[j] = jnp.where(lane_st < HK, d2[:HVA, :], d2[HVA:, :])

    for jc in range(ncx):
        sl = slice(jc * CHUNK, (jc + 1) * CHUNK)
        delta_state(kkc_ref[sl, :], vtc_ref[:, sl], jc)

    def prep_body(jl, carry):
        t0 = pl.multiple_of(jl * CHUNK, CHUNK)
        delta_state(kk_ref[pl.ds(t0, CHUNK), :], vt_ref[:, pl.ds(t0, CHUNK)], jl + ncx)
        return carry

    lax.fori_loop(0, ncl, prep_body, 0, unroll=2)

    def scan_step(jf, jb, state, m):
        stf_s[jf] = state.astype(BF16)
        stb_s[jb] = state.astype(BF16)
        m0f_s[pl.ds(pl.multiple_of(jf * SUB, SUB), SUB), :] = m
        m0b_s[pl.ds(pl.multiple_of(jb * SUB, SUB), SUB), :] = m
        cmax = jnp.where(row == 0, tile(cmax_s, jf), tile(cmax_s, jb))
        btot = jnp.where(row == 0, tile(btot_s, jf), tile(btot_s, jb))
        g = jnp.maximum(m, cmax)
        a = jnp.exp(m - g)
        bc = jnp.exp(cmax - g)
        a_l = jnp.where(lane1 < HK, a[0:1, :], a[1:2, :])
        b_l = jnp.where(lane1 < HK, bc[0:1, :], bc[1:2, :])
        dc = jnp.where(lane_st < HK, dc_s[jf], dc_s[jb])
        return a_l * state + b_l * dc, btot + g

    state = jnp.zeros((HVA, CHUNK), F32)
    m = jnp.zeros((SUB, CHUNK), F32)
    for p in range(ncx):
        state, m = scan_step(p, ncx - 1 - p, state, m)

    def scan_body(p, carry):
        return scan_step(ncx + p, nc - 1 - p, *carry)

    lax.fori_loop(0, ncl, scan_body, (state, m))

    si = lax.broadcasted_iota(jnp.int32, (CHUNK, CHUNK), 0)
    ti = lax.broadcasted_iota(jnp.int32, (CHUNK, CHUNK), 1)
    masks = (si <= ti, si >= ti)
    hng = hng_ref[...]
    zpad = jnp.zeros((CHUNK - SUB, CHUNK), F32)
    zq = jnp.zeros((HK, CHUNK), BF16)

    def stabiliser(j):
        m0 = jnp.where(row == 0, tile(m0f_s, j), tile(m0b_s, j))
        return m0, jnp.maximum(m0, tile(cm_s, j))

    def weights_stage(jl):
        j = jl + ncx
        t0 = pl.multiple_of(jl * CHUNK, CHUNK)
        m0, g = stabiliser(j)
        wi = jnp.exp(m0 - g)
        ct = jnp.concatenate([tile(c_s, j), zpad], axis=0).T
        qt = qt_ref[:, pl.ds(t0, CHUNK)]
        st = lax.dot_general(kk_ref[pl.ds(t0, CHUNK), :], jnp.concatenate([qt, zq], axis=0),
                             (((1,), (0,)), ((), ())), preferred_element_type=F32)
        qtf = qt.astype(F32)
        for r in range(2):
            w = jnp.where(masks[r], jnp.exp(ct[:, r:r + 1] - g[r:r + 1, :]), 0.0)
            rhs_s[jl, 0:CHUNK, r * CHUNK:(r + 1) * CHUNK] = (st * w).astype(BF16)
            qw = (qtf * wi[r:r + 1, :]).astype(BF16)
            rhs_s[jl, CHUNK + r * HK:CHUNK + (r + 1) * HK, r * CHUNK:(r + 1) * CHUNK] = qw
            rhs_s[jl, CHUNK + (1 - r) * HK:CHUNK + (2 - r) * HK, r * CHUNK:(r + 1) * CHUNK] = zq

    def output_stage(jl):
        j = jl + ncx
        t0 = pl.multiple_of(jl * CHUNK, CHUNK)
        _, g = stabiliser(j)
        fl = jnp.exp(-(tile(bb_s, j) + g))
        vta = jnp.concatenate([vt_ref[:, pl.ds(t0, CHUNK)], aug_rows], axis=0)
        stt = jnp.where(lane_st < HK, stf_s[j], stb_s[j])
        nd = jnp.dot(jnp.concatenate([vta, stt], axis=1), rhs_s[jl], preferred_element_type=F32)
        ht = None
        for r in range(2):
            blk = nd[:, r * CHUNK:(r + 1) * CHUNK]
            rden = 1.0 / jnp.maximum(jnp.abs(blk[HV:HV + 1, :]), fl[r:r + 1, :])
            part = blk[:HV, :] * rden
            ht = part if ht is None else ht + part
        mu = jnp.mean(ht, axis=0, keepdims=True)
        xc = ht - mu
        var = jnp.mean(xc * xc, axis=0, keepdims=True)
        y = (xc * lax.rsqrt(var + LN_EPS)).T
        gate = _sigmoid(o_ref[pl.ds(t0, CHUNK), :].astype(F32))
        yb_ref[pl.ds(t0, CHUNK), :] = (y * hng * gate).astype(BF16)

    for u in range(OUT_GROUP):
        weights_stage(u)

    def out_body(i, carry):
        for u in range(OUT_GROUP):
            output_stage(i * OUT_GROUP + u)
        for u in range(OUT_GROUP):
            weights_stage((i + 1) * OUT_GROUP + u)
        return carry

    lax.fori_loop(0, ncl // OUT_GROUP - 1, out_body, 0)
    for u in range(OUT_GROUP):
        output_stage(ncl - OUT_GROUP + u)


def _mlstm(kk, qt, vt, o, gr, kkc, vtc, grc, hng):
    bsz, s, _ = kk.shape
    sc = kkc.shape[1]
    ncl = s // CHUNK
    ncx = sc // CHUNK
    nc = ncl + ncx
    head = lambda n: pl.BlockSpec((None, n, CHUNK), lambda b, h: (b, 0, h))
    head_t = lambda rows, n: pl.BlockSpec((None, rows, n), lambda b, h: (b, h, 0))
    gates = lambda n: pl.BlockSpec((None, None, 2, n * SUB, CHUNK), lambda b, h: (b, h, 0, 0, 0))
    row_tiles = lambda n: pltpu.VMEM((n * SUB, CHUNK), F32)
    return pl.pallas_call(
        _mlstm_kernel,
        grid=(bsz, H_B),
        in_specs=[head(s), head_t(HK, s), head_t(HV, s), head(s), gates(ncl),
                  head(sc), head_t(HV, sc), gates(ncx),
                  pl.BlockSpec((1, CHUNK), lambda b, h: (0, h))],
        out_specs=head(s),
        out_shape=jax.ShapeDtypeStruct((bsz, s, D_MLSTM), BF16),
        scratch_shapes=[row_tiles(nc), row_tiles(nc), row_tiles(nc), row_tiles(nc), row_tiles(nc),
                        pltpu.VMEM((nc, HVA, CHUNK), F32),
                        pltpu.VMEM((nc, HVA, CHUNK), BF16), pltpu.VMEM((nc, HVA, CHUNK), BF16),
                        row_tiles(nc), row_tiles(nc),
                        pltpu.VMEM((ncl, 2 * CHUNK, 2 * CHUNK), BF16)],
        compiler_params=pltpu.CompilerParams(
            dimension_semantics=("parallel", "arbitrary"), vmem_limit_bytes=VMEM_LIMIT_BYTES),
        name="mlstm",
    )(kk, qt, vt, o, gr, kkc, vtc, grc, hng)


def _post_kernel(x_ref, ya_ref, yb_ref, g1_ref, sh2_ref, sc2_ref, g2_ref, wo_ref, l1g_ref, l1b_ref,
                 w1_ref, b1_ref, w2_ref, b2_ref, l2g_ref, l2b_ref, out_ref):
    nf = 1024
    nparts = D_FF // nf
    nsub = x_ref.shape[0] // POST_SUB

    def head(r):
        rows = slice(r * POST_SUB, (r + 1) * POST_SUB)
        yab = jnp.concatenate([ya_ref[rows, :], yb_ref[rows, :]], axis=1)
        y = jnp.dot(yab, wo_ref[...], preferred_element_type=F32)
        x1 = _ln_stats(ALPHA * x_ref[rows, :] + g1_ref[...] * y) * l1g_ref[...] + l1b_ref[...]
        h2 = (_ln_stats(x1) * (1.0 + sc2_ref[...]) + sh2_ref[...]).astype(BF16)
        return x1, h2

    def mlp_part(h2, f):
        cols = slice(f * nf, (f + 1) * nf)
        t = jnp.maximum(jnp.dot(h2, w1_ref[:, cols], preferred_element_type=F32) + b1_ref[:, cols], 0.0)
        return jnp.dot((t * t).astype(BF16), w2_ref[cols, :], preferred_element_type=F32)

    def tail(r, x1, z):
        rows = slice(r * POST_SUB, (r + 1) * POST_SUB)
        out_ref[rows, :] = _ln_stats(ALPHA * x1 + g2_ref[...] * (z + b2_ref[...])) * l2g_ref[...] + l2b_ref[...]

    cur = head(0)
    pending = None
    for r in range(nsub):
        x1, h2 = cur
        z = mlp_part(h2, 0)
        if pending is not None:
            tail(*pending)
        z = z + mlp_part(h2, 1)
        if r + 1 < nsub:
            cur = head(r + 1)
        for f in range(2, nparts):
            z = z + mlp_part(h2, f)
        pending = (r, x1, z)
    tail(*pending)


def _post(x, ya, yb, g1, sh2, sc2, g2, wts, tm):
    bsz, s, d = x.shape
    (wo, l1g, l1b, w1, b1, w2, b2, l2g, l2b) = wts
    vec_spec = pl.BlockSpec((None, 1, d), lambda b, i: (b, 0, 0))
    tok_spec = lambda n: pl.BlockSpec((None, tm, n), lambda b, i: (b, i, 0))
    return pl.pallas_call(
        _post_kernel,
        grid=(bsz, s // tm),
        in_specs=[tok_spec(d), tok_spec(D_GMLP), tok_spec(D_MLSTM), vec_spec, vec_spec, vec_spec, vec_spec]
                 + [_const_spec(w.shape) for w in wts],
        out_specs=tok_spec(d),
        out_shape=jax.ShapeDtypeStruct((bsz, s, d), F32),
        compiler_params=pltpu.CompilerParams(
            dimension_semantics=("parallel", "arbitrary"), vmem_limit_bytes=VMEM_LIMIT_BYTES),
        name="post",
    )(x, ya, yb, g1, sh2, sc2, g2, wo, l1g, l1b, w1, b1, w2, b2, l2g, l2b)


def _proj_weights(w_in, w_s, b_s, ln_v_g, ln_v_b, conv_qk, b_gates):
    q0 = 2 * D_GMLP
    v0 = q0 + 2 * D_QK
    o0 = v0 + D_MLSTM
    g0 = o0 + D_MLSTM
    def gate_tiles(a):
        a = a.reshape(2, 2, H_B, -1).transpose(2, 1, 0, 3)
        a = jnp.pad(a, ((0, 0), (0, 0), (0, SUB - 2), (0, 0)))
        return a.reshape(H_B * 2 * SUB, -1)

    wuv = w_in[:, :q0].astype(BF16)
    wqk = w_in[:, q0:v0].astype(BF16)
    wvt = jnp.concatenate([w_in[:, v0:o0].T, gate_tiles(w_in[:, g0:].T)], axis=0).astype(BF16)
    wo = w_in[:, o0:g0].astype(BF16)
    ws = w_s.astype(BF16)
    bs = jnp.broadcast_to(b_s[:, :, None], (G_A, CHUNK, CHUNK))
    qs = jnp.concatenate([jnp.full((D_QK,), HK ** -0.5, F32), jnp.ones((D_QK,), F32)])[None, :]
    return (wuv, wqk, wvt, wo, ws, bs, ln_v_g[None, :], ln_v_b[None, :], conv_qk, qs, gate_tiles(b_gates[:, None]))


def kernel(x, c, ctx, c_ctx, w_ada, b_ada, w_in, w_s, b_s, ln_v_g, ln_v_b, conv_qk, b_gates, hn_g,
           w_out, ln1_g, ln1_b, w1, b1, w2, b2, ln2_g, ln2_b):
    bsz = x.shape[0]
    l = 0
    cond = jnp.concatenate([c, c_ctx[None, :], jnp.zeros((16 - bsz - 1, D_MODEL), F32)], axis=0)
    mods = _ada(cond, w_ada[l], b_ada[l][None, :])
    lat = [mods[:bsz, None, i * D_MODEL:(i + 1) * D_MODEL] for i in range(6)]
    cx = [jnp.broadcast_to(mods[bsz, i * D_MODEL:(i + 1) * D_MODEL], (bsz, 1, D_MODEL)) for i in range(2)]
    sh1, sc1, g1, sh2, sc2, g2 = lat

    pw = _proj_weights(w_in[l], w_s[l], b_s[l], ln_v_g[l], ln_v_b[l], conv_qk[l], b_gates[l])
    ya, kk, qt, vt, o, gr = _proj(x, sh1, sc1, pw, PROJ_TILE)
    _, kkc, _, vtc, _, grc = _proj(ctx, cx[0], cx[1], pw, ctx.shape[1])

    yb = _mlstm(kk, qt, vt, o, gr, kkc, vtc, grc, hn_g[l][None, :])

    row = lambda v: v[None, :]
    post_w = (w_out[l].astype(BF16), row(ln1_g[l]), row(ln1_b[l]), w1[l].astype(BF16), row(b1[l]),
              w2[l].astype(BF16), row(b2[l]), row(ln2_g[l]), row(ln2_b[l]))
    return _post(x, ya, yb, g1, sh2, sc2, g2, post_w, POST_TILE)
```

```python
import jax
import jax.numpy as jnp
from jax import lax
from jax.experimental import pallas as pl
from jax.experimental.pallas import tpu as pltpu

F32 = jnp.float32
BF16 = jnp.bfloat16

D_MODEL = 1024
D_GMLP = 512
G_A = 4
CHUNK = 128
D_MLSTM = 512
H_B = 4
HV = 128
HK = 64
D_QK = 256
D_FF = 4096
DEPTH = 1
ALPHA = (2 * DEPTH) ** 0.25
LN_EPS = 1e-5

HALO = 16
SUB = 8
N_GATE = 4
Q_C, Q_CM, Q_B, Q_CMAX, Q_BTOT, Q_M0, N_GQ = 0, 2, 4, 6, 8, 10, 12
AUG = 16
HVA = HV + AUG
VMEM_LIMIT_BYTES = 56 * 1024 * 1024
PROJ_TILE = 1024
PROJ_SUB = 256
POST_TILE = 1024
POST_SUB = 256
OUT_GROUP = 8

NT_DIMS = (((1,), (1,)), ((), ()))


def _ln_stats(x):
    mu = jnp.mean(x, axis=-1, keepdims=True)
    xc = x - mu
    var = jnp.mean(xc * xc, axis=-1, keepdims=True)
    return xc * lax.rsqrt(var + LN_EPS)


def _sigmoid(x):
    return 1.0 / (1.0 + jnp.exp(-x))


def _const_spec(shape):
    nd = len(shape)
    return pl.BlockSpec(shape, lambda *_: (0,) * nd, pipeline_mode=pl.Buffered(1))


def _ada_kernel(c_ref, w_ref, b_ref, o_ref):
    c = c_ref[...]
    s = (c * _sigmoid(c)).astype(BF16)
    o_ref[...] = jnp.dot(s, w_ref[...].astype(BF16), preferred_element_type=F32) + b_ref[...]


def _ada(cond, w, b):
    m, d = cond.shape
    n = w.shape[1]
    bn = 1536
    return pl.pallas_call(
        _ada_kernel,
        grid=(n // bn,),
        in_specs=[pl.BlockSpec((m, d), lambda j: (0, 0)),
                  pl.BlockSpec((d, bn), lambda j: (0, j)),
                  pl.BlockSpec((1, bn), lambda j: (0, j))],
        out_specs=pl.BlockSpec((m, bn), lambda j: (0, j)),
        out_shape=jax.ShapeDtypeStruct((m, n), F32),
        name="ada",
    )(cond, w, b)


def _proj_kernel(x_ref, xp_ref, xn_ref, sh_ref, sc_ref, wuv_ref, wqk_ref, wvt_ref, wo_ref,
                 ws_ref, bs_ref, lnvg_ref, lnvb_ref, cw_ref, qs_ref, bg_ref,
                 ya_ref, kk_ref, qt_ref, vt_ref, o_ref, gr_ref):
    i = pl.program_id(1)
    nt = pl.num_programs(1)
    tm = x_ref.shape[0]
    ts = min(PROJ_SUB, tm)
    nsub = tm // ts
    sc1 = 1.0 + sc_ref[...]
    sh = sh_ref[...]
    cw = cw_ref[...]
    lane = lax.broadcasted_iota(jnp.int32, (ts, CHUNK), 1)

    def mod(xv):
        return _ln_stats(xv) * sc1 + sh

    def project(s, hm, hext):
        r0 = s * ts
        rows = slice(r0, r0 + ts)
        puv = jnp.dot(hm, wuv_ref[...], preferred_element_type=F32)
        pqk = jnp.dot(hext, wqk_ref[...], preferred_element_type=F32)
        vtg = lax.dot_general(wvt_ref[...], hm, NT_DIMS, preferred_element_type=F32)
        vt_ref[:, rows] = vtg[:D_MLSTM, :].astype(BF16)
        o_ref[rows, :] = jnp.dot(hm, wo_ref[...], preferred_element_type=F32).astype(BF16)

        conv = (pltpu.roll(pqk, 1, 0) * cw[0:1, :] + pqk * cw[1:2, :]
                + pltpu.roll(pqk, ts + 2 * HALO - 1, 0) * cw[2:3, :])
        conv = conv[HALO:HALO + ts, :]
        act = conv * _sigmoid(conv) * qs_ref[...]
        for pair in range(H_B // 2):
            blk = act[:, D_QK + pair * CHUNK:D_QK + (pair + 1) * CHUNK]
            swp = pltpu.roll(blk, HK, 1)
            kk_ref[rows, (2 * pair) * CHUNK:(2 * pair + 1) * CHUNK] = jnp.where(lane < HK, blk, swp).astype(BF16)
            kk_ref[rows, (2 * pair + 1) * CHUNK:(2 * pair + 2) * CHUNK] = jnp.where(lane < HK, swp, blk).astype(BF16)
            for c in range(ts // CHUNK):
                blk = act[c * CHUNK:(c + 1) * CHUNK, pair * CHUNK:(pair + 1) * CHUNK]
                qt_ref[pair * CHUNK:(pair + 1) * CHUNK, r0 + c * CHUNK:r0 + (c + 1) * CHUNK] = blk.T.astype(BF16)

        gt = vtg[D_MLSTM:, :] + bg_ref[...]
        for h in range(H_B):
            for q in range(N_GATE):
                for c in range(ts // CHUNK):
                    jr = r0 // CHUNK + c
                    gr_ref[h, q, jr:jr + 1, :] = gt[h * N_GATE + q:h * N_GATE + q + 1, c * CHUNK:(c + 1) * CHUNK]

        lnv = (_ln_stats(puv[:, D_GMLP:]) * lnvg_ref[...] + lnvb_ref[...]).astype(BF16)
        nch = ts // CHUNK
        for g in range(G_A):
            cols = slice(g * CHUNK, (g + 1) * CHUNK)
            side = jnp.concatenate([lnv[c * CHUNK:(c + 1) * CHUNK, cols] for c in range(nch)], axis=1)
            mixed = jnp.dot(ws_ref[g], side, preferred_element_type=F32)
            for c in range(nch):
                m_c = mixed[:, c * CHUNK:(c + 1) * CHUNK] + bs_ref[g]
                ya_ref[r0 + c * CHUNK:r0 + (c + 1) * CHUNK, cols] = (
                    puv[c * CHUNK:(c + 1) * CHUNK, cols] * m_c).astype(BF16)

    if gr_ref.shape[2] > tm // CHUNK:
        gr_ref[:, :, tm // CHUNK:, :] = jnp.zeros((H_B, N_GATE, gr_ref.shape[2] - tm // CHUNK, CHUNK), F32)
    keep_prev = jnp.where(i > 0, 1.0, 0.0)
    keep_next = jnp.where(i < nt - 1, 1.0, 0.0)
    left = (mod(xp_ref[...]) * keep_prev).astype(BF16)
    cur = mod(x_ref[0:ts, :]).astype(BF16)
    for s in range(nsub):
        if s + 1 < nsub:
            nxt = mod(x_ref[(s + 1) * ts:(s + 2) * ts, :]).astype(BF16)
        else:
            nxt = (mod(xn_ref[...]) * keep_next).astype(BF16)
        project(s, cur, jnp.concatenate([left, cur, nxt[:HALO, :]], axis=0))
        left, cur = cur[ts - HALO:, :], nxt


def _proj(x, sh, sc, wts, tm):
    bsz, s, d = x.shape
    nt = s // tm
    hb = tm // HALO
    nhb = s // HALO
    (wuv, wqk, wvt, wo, ws, bs, lnvg, lnvb, cw, qs, bg) = wts
    bg_t = jnp.broadcast_to(bg, (bg.shape[0], min(PROJ_SUB, tm)))
    tok = lambda n: jax.ShapeDtypeStruct((bsz, s, n), BF16)
    tok_spec = lambda n: pl.BlockSpec((None, tm, n), lambda b, i: (b, i, 0))
    tr = lambda n: jax.ShapeDtypeStruct((bsz, n, s), BF16)
    tr_spec = lambda n: pl.BlockSpec((None, n, tm), lambda b, i: (b, 0, i))
    vec_spec = pl.BlockSpec((None, 1, d), lambda b, i: (b, 0, 0))
    return pl.pallas_call(
        _proj_kernel,
        grid=(bsz, nt),
        in_specs=[
            pl.BlockSpec((None, tm, d), lambda b, i: (b, i, 0)),
            pl.BlockSpec((None, HALO, d), lambda b, i: (b, jnp.maximum(i * hb - 1, 0), 0)),
            pl.BlockSpec((None, HALO, d), lambda b, i: (b, jnp.minimum((i + 1) * hb, nhb - 1), 0)),
            vec_spec, vec_spec,
            _const_spec(wuv.shape), _const_spec(wqk.shape), _const_spec(wvt.shape), _const_spec(wo.shape),
            _const_spec(ws.shape), _const_spec(bs.shape), _const_spec(lnvg.shape), _const_spec(lnvb.shape),
            _const_spec(cw.shape), _const_spec(qs.shape), _const_spec(bg_t.shape),
        ],
        out_specs=[tok_spec(D_GMLP), tok_spec(2 * D_QK), tr_spec(D_QK), tr_spec(D_MLSTM), tok_spec(D_MLSTM),
                   pl.BlockSpec((None, H_B, N_GATE, max(tm // CHUNK, SUB), CHUNK), lambda b, i: (b, 0, 0, i, 0))],
        out_shape=[tok(D_GMLP), tok(2 * D_QK), tr(D_QK), tr(D_MLSTM), tok(D_MLSTM),
                   jax.ShapeDtypeStruct((bsz, H_B, N_GATE, max(s // CHUNK, SUB), CHUNK), F32)],
        compiler_params=pltpu.CompilerParams(
            dimension_semantics=("parallel", "arbitrary"), vmem_limit_bytes=VMEM_LIMIT_BYTES),
        name="proj",
    )(x, x, x, sh, sc, wuv, wqk, wvt, wo, ws, bs, lnvg, lnvb, cw, qs, bg_t)


def _prefix(x, op, lane, fill):
    d = 1
    while d < CHUNK:
        x = op(x, jnp.where(lane >= d, pltpu.roll(x, d, 1), fill))
        d *= 2
    return x


def _suffix(x, op, lane, fill):
    d = 1
    while d < CHUNK:
        x = op(x, jnp.where(lane < CHUNK - d, pltpu.roll(x, CHUNK - d, 1), fill))
        d *= 2
    return x


def _mlstm_kernel(kk_ref, qt_ref, vt_ref, o_ref, gr_ref, kkc_ref, vtc_ref, grc_ref, hng_ref, yb_ref,
                  gq_s, dc_s, stf_s, stb_s, rhs_s):
    ncx = kkc_ref.shape[0] // CHUNK
    ncl = kk_ref.shape[0] // CHUNK
    lat0 = grc_ref.shape[1]
    aug_rows = jnp.where(lax.broadcasted_iota(jnp.int32, (AUG, CHUNK), 0) == 0, 1.0, 0.0).astype(BF16)
    row = lax.broadcasted_iota(jnp.int32, (SUB, CHUNK), 0)
    lane1 = lax.broadcasted_iota(jnp.int32, (1, CHUNK), 1)
    lane_st = lax.broadcasted_iota(jnp.int32, (HVA, CHUNK), 1)

    def vec(q, j):
        return gq_s[q, pl.ds(j, 1), :]

    gates = [jnp.concatenate([grc_ref[q], gr_ref[q]], axis=0) for q in range(N_GATE)]
    lane = lax.broadcasted_iota(jnp.int32, gates[0].shape, 1)
    scans = ((_prefix, 0), (_suffix, 1))
    for scan, r in scans:
        gf = gates[2 + r]
        lf = jnp.minimum(gf, 0.0) - jnp.log(1.0 + jnp.exp(-jnp.abs(gf)))
        b = scan(lf, jnp.add, lane, 0.0)
        c = gates[r] - b
        gq_s[Q_C + r] = c
        gq_s[Q_CM + r] = scan(c, jnp.maximum, lane, -jnp.inf)
        gq_s[Q_B + r] = b
        gq_s[Q_CMAX + r] = jnp.broadcast_to(jnp.max(c, axis=1, keepdims=True), c.shape)
        gq_s[Q_BTOT + r] = jnp.broadcast_to(jnp.min(b, axis=1, keepdims=True), c.shape)

    def delta_state(kk, vt, j):
        vta = jnp.concatenate([vt, aug_rows], axis=0).astype(F32)
        lhs = jnp.concatenate([vta * jnp.exp(vec(Q_C + r, j) - vec(Q_CMAX + r, j)) for r in range(2)],
                              axis=0).astype(BF16)
        d2 = jnp.dot(lhs, kk, preferred_element_type=F32)
        dc_s[j] = jnp.where(lane_st < HK, d2[:HVA, :], d2[HVA:, :])

    for jc in range(ncx):
        sl = slice(jc * CHUNK, (jc + 1) * CHUNK)
        delta_state(kkc_ref[sl, :], vtc_ref[:, sl], jc)

    def prep_body(jl, carry):
        t0 = pl.multiple_of(jl * CHUNK, CHUNK)
        delta_state(kk_ref[pl.ds(t0, CHUNK), :], vt_ref[:, pl.ds(t0, CHUNK)], jl + lat0)
        return carry

    lax.fori_loop(0, ncl, prep_body, 0, unroll=4)

    def scan_step(js, state, ms):
        stf_s[js[0]] = state.astype(BF16)
        stb_s[js[1]] = state.astype(BF16)
        coef, new_ms = [], []
        for r in range(2):
            gq_s[Q_M0 + r, pl.ds(js[r], 1), :] = ms[r]
            cmax = vec(Q_CMAX + r, js[r])
            g = jnp.maximum(ms[r], cmax)
            coef.append((jnp.exp(ms[r] - g), jnp.exp(cmax - g)))
            new_ms.append(vec(Q_BTOT + r, js[r]) + g)
        a_l = jnp.where(lane1 < HK, coef[0][0], coef[1][0])
        b_l = jnp.where(lane1 < HK, coef[0][1], coef[1][1])
        dc = jnp.where(lane_st < HK, dc_s[js[0]], dc_s[js[1]])
        return a_l * state + b_l * dc, tuple(new_ms)

    state = jnp.zeros((HVA, CHUNK), F32)
    ms = (jnp.zeros((1, CHUNK), F32), jnp.zeros((1, CHUNK), F32))
    for p in range(ncx):
        state, ms = scan_step((p, ncx - 1 - p), state, ms)

    def scan_body(p, carry):
        return scan_step((lat0 + p, lat0 + ncl - 1 - p), *carry)

    lax.fori_loop(0, ncl, scan_body, (state, ms), unroll=2)

    si = lax.broadcasted_iota(jnp.int32, (CHUNK, CHUNK), 0)
    ti = lax.broadcasted_iota(jnp.int32, (CHUNK, CHUNK), 1)
    masks = (si <= ti, si >= ti)
    hng = hng_ref[...]
    zpad = jnp.zeros((CHUNK - SUB, CHUNK), F32)
    zq = jnp.zeros((HK, CHUNK), BF16)

    def stabiliser(r, j):
        m0 = vec(Q_M0 + r, j)
        return m0, jnp.maximum(m0, vec(Q_CM + r, j))

    def weights_stage(jl):
        j = jl + lat0
        t0 = pl.multiple_of(jl * CHUNK, CHUNK)
        c2 = jnp.where(row == 0, vec(Q_C, j), jnp.where(row == 1, vec(Q_C + 1, j), 0.0))
        ct = jnp.concatenate([c2, zpad], axis=0).T
        qt = qt_ref[:, pl.ds(t0, CHUNK)]
        st = lax.dot_general(kk_ref[pl.ds(t0, CHUNK), :], jnp.concatenate([qt, zq], axis=0),
                             (((1,), (0,)), ((), ())), preferred_element_type=F32)
        qtf = qt.astype(F32)
        for r in range(2):
            m0, g = stabiliser(r, j)
            w = jnp.where(masks[r], jnp.exp(ct[:, r:r + 1] - g), 0.0)
            rhs_s[jl, 0:CHUNK, r * CHUNK:(r + 1) * CHUNK] = (st * w).astype(BF16)
            qw = (qtf * jnp.exp(m0 - g)).astype(BF16)
            rhs_s[jl, CHUNK + r * HK:CHUNK + (r + 1) * HK, r * CHUNK:(r + 1) * CHUNK] = qw
            rhs_s[jl, CHUNK + (1 - r) * HK:CHUNK + (2 - r) * HK, r * CHUNK:(r + 1) * CHUNK] = zq

    def output_stage(jl):
        j = jl + lat0
        t0 = pl.multiple_of(jl * CHUNK, CHUNK)
        vta = jnp.concatenate([vt_ref[:, pl.ds(t0, CHUNK)], aug_rows], axis=0)
        stt = jnp.where(lane_st < HK, stf_s[j], stb_s[j])
        nd = jnp.dot(jnp.concatenate([vta, stt], axis=1), rhs_s[jl], preferred_element_type=F32)
        ht = None
        for r in range(2):
            blk = nd[:, r * CHUNK:(r + 1) * CHUNK]
            fl = jnp.exp(-(vec(Q_B + r, j) + stabiliser(r, j)[1]))
            rden = 1.0 / jnp.maximum(jnp.abs(blk[HV:HV + 1, :]), fl)
            part = blk[:HV, :] * rden
            ht = part if ht is None else ht + part
        mu = jnp.mean(ht, axis=0, keepdims=True)
        xc = ht - mu
        var = jnp.mean(xc * xc, axis=0, keepdims=True)
        y = (xc * lax.rsqrt(var + LN_EPS)).T
        gate = _sigmoid(o_ref[pl.ds(t0, CHUNK), :].astype(F32))
        yb_ref[pl.ds(t0, CHUNK), :] = (y * hng * gate).astype(BF16)

    for u in range(OUT_GROUP):
        weights_stage(u)

    def out_body(i, carry):
        for u in range(OUT_GROUP):
            output_stage(i * OUT_GROUP + u)
        for u in range(OUT_GROUP):
            weights_stage((i + 1) * OUT_GROUP + u)
        return carry

    lax.fori_loop(0, ncl // OUT_GROUP - 1, out_body, 0)
    for u in range(OUT_GROUP):
        output_stage(ncl - OUT_GROUP + u)


def _mlstm(kk, qt, vt, o, gr, kkc, vtc, grc, hng):
    bsz, s, _ = kk.shape
    sc = kkc.shape[1]
    ncl = s // CHUNK
    nr = grc.shape[3] + gr.shape[3]
    head = lambda n: pl.BlockSpec((None, n, CHUNK), lambda b, h: (b, 0, h))
    head_t = lambda rows, n: pl.BlockSpec((None, rows, n), lambda b, h: (b, h, 0))
    gates = lambda a: pl.BlockSpec((None, None, N_GATE, a.shape[3], CHUNK), lambda b, h: (b, h, 0, 0, 0))
    return pl.pallas_call(
        _mlstm_kernel,
        grid=(bsz, H_B),
        in_specs=[head(s), head_t(HK, s), head_t(HV, s), head(s), gates(gr),
                  head(sc), head_t(HV, sc), gates(grc),
                  pl.BlockSpec((1, CHUNK), lambda b, h: (0, h))],
        out_specs=head(s),
        out_shape=jax.ShapeDtypeStruct((bsz, s, D_MLSTM), BF16),
        scratch_shapes=[pltpu.VMEM((N_GQ, nr, CHUNK), F32),
                        pltpu.VMEM((nr, HVA, CHUNK), F32),
                        pltpu.VMEM((nr, HVA, CHUNK), BF16), pltpu.VMEM((nr, HVA, CHUNK), BF16),
                        pltpu.VMEM((ncl, 2 * CHUNK, 2 * CHUNK), BF16)],
        compiler_params=pltpu.CompilerParams(
            dimension_semantics=("parallel", "arbitrary"), vmem_limit_bytes=VMEM_LIMIT_BYTES),
        name="mlstm",
    )(kk, qt, vt, o, gr, kkc, vtc, grc, hng)


def _post_kernel(x_ref, ya_ref, yb_ref, g1_ref, sh2_ref, sc2_ref, g2_ref, wo_ref, l1g_ref, l1b_ref,
                 w1_ref, b1_ref, w2_ref, b2_ref, l2g_ref, l2b_ref, out_ref):
    nf = 1024
    nparts = D_FF // nf
    nsub = x_ref.shape[0] // POST_SUB

    def head(r):
        rows = slice(r * POST_SUB, (r + 1) * POST_SUB)
        yab = jnp.concatenate([ya_ref[rows, :], yb_ref[rows, :]], axis=1)
        y = jnp.dot(yab, wo_ref[...], preferred_element_type=F32)
        x1 = _ln_stats(ALPHA * x_ref[rows, :] + g1_ref[...] * y) * l1g_ref[...] + l1b_ref[...]
        h2 = (_ln_stats(x1) * (1.0 + sc2_ref[...]) + sh2_ref[...]).astype(BF16)
        return x1, h2

    def mlp_part(h2, f):
        cols = slice(f * nf, (f + 1) * nf)
        t = jnp.maximum(jnp.dot(h2, w1_ref[:, cols], preferred_element_type=F32) + b1_ref[:, cols], 0.0)
        return jnp.dot((t * t).astype(BF16), w2_ref[cols, :], preferred_element_type=F32)

    def tail(r, x1, z):
        rows = slice(r * POST_SUB, (r + 1) * POST_SUB)
        out_ref[rows, :] = _ln_stats(ALPHA * x1 + g2_ref[...] * (z + b2_ref[...])) * l2g_ref[...] + l2b_ref[...]

    cur = head(0)
    pending = None
    for r in range(nsub):
        x1, h2 = cur
        z = mlp_part(h2, 0)
        if pending is not None:
            tail(*pending)
        z = z + mlp_part(h2, 1)
        if r + 1 < nsub:
            cur = head(r + 1)
        for f in range(2, nparts):
            z = z + mlp_part(h2, f)
        pending = (r, x1, z)
    tail(*pending)


def _post(x, ya, yb, g1, sh2, sc2, g2, wts, tm):
    bsz, s, d = x.shape
    (wo, l1g, l1b, w1, b1, w2, b2, l2g, l2b) = wts
    vec_spec = pl.BlockSpec((None, 1, d), lambda b, i: (b, 0, 0))
    tok_spec = lambda n: pl.BlockSpec((None, tm, n), lambda b, i: (b, i, 0))
    return pl.pallas_call(
        _post_kernel,
        grid=(bsz, s // tm),
        in_specs=[tok_spec(d), tok_spec(D_GMLP), tok_spec(D_MLSTM), vec_spec, vec_spec, vec_spec, vec_spec]
                 + [_const_spec(w.shape) for w in wts],
        out_specs=tok_spec(d),
        out_shape=jax.ShapeDtypeStruct((bsz, s, d), F32),
        compiler_params=pltpu.CompilerParams(
            dimension_semantics=("parallel", "arbitrary"), vmem_limit_bytes=VMEM_LIMIT_BYTES),
        name="post",
    )(x, ya, yb, g1, sh2, sc2, g2, wo, l1g, l1b, w1, b1, w2, b2, l2g, l2b)


def _proj_weights(w_in, w_s, b_s, ln_v_g, ln_v_b, conv_qk, b_gates):
    q0 = 2 * D_GMLP
    v0 = q0 + 2 * D_QK
    o0 = v0 + D_MLSTM
    g0 = o0 + D_MLSTM
    def gate_tiles(a):
        return a.reshape(2, 2, H_B, -1).transpose(2, 1, 0, 3).reshape(H_B * N_GATE, -1)

    wuv = w_in[:, :q0].astype(BF16)
    wqk = w_in[:, q0:v0].astype(BF16)
    wvt = jnp.concatenate([w_in[:, v0:o0].T, gate_tiles(w_in[:, g0:].T)], axis=0).astype(BF16)
    wo = w_in[:, o0:g0].astype(BF16)
    ws = w_s.astype(BF16)
    bs = jnp.broadcast_to(b_s[:, :, None], (G_A, CHUNK, CHUNK))
    qs = jnp.concatenate([jnp.full((D_QK,), HK ** -0.5, F32), jnp.ones((D_QK,), F32)])[None, :]
    return (wuv, wqk, wvt, wo, ws, bs, ln_v_g[None, :], ln_v_b[None, :], conv_qk, qs, gate_tiles(b_gates[:, None]))


def kernel(x, c, ctx, c_ctx, w_ada, b_ada, w_in, w_s, b_s, ln_v_g, ln_v_b, conv_qk, b_gates, hn_g,
           w_out, ln1_g, ln1_b, w1, b1, w2, b2, ln2_g, ln2_b):
    bsz = x.shape[0]
    l = 0
    cond = jnp.concatenate([c, c_ctx[None, :], jnp.zeros((16 - bsz - 1, D_MODEL), F32)], axis=0)
    mods = _ada(cond, w_ada[l], b_ada[l][None, :])
    lat = [mods[:bsz, None, i * D_MODEL:(i + 1) * D_MODEL] for i in range(6)]
    cx = [jnp.broadcast_to(mods[bsz, i * D_MODEL:(i + 1) * D_MODEL], (bsz, 1, D_MODEL)) for i in range(2)]
    sh1, sc1, g1, sh2, sc2, g2 = lat

    pw = _proj_weights(w_in[l], w_s[l], b_s[l], ln_v_g[l], ln_v_b[l], conv_qk[l], b_gates[l])
    ya, kk, qt, vt, o, gr = _proj(x, sh1, sc1, pw, PROJ_TILE)
    _, kkc, _, vtc, _, grc = _proj(ctx, cx[0], cx[1], pw, ctx.shape[1])

    yb = _mlstm(kk, qt, vt, o, gr, kkc, vtc, grc, hn_g[l][None, :])

    row = lambda v: v[None, :]
    post_w = (w_out[l].astype(BF16), row(ln1_g[l]), row(ln1_b[l]), w1[l].astype(BF16), row(b1[l]),
              w2[l].astype(BF16), row(b2[l]), row(ln2_g[l]), row(ln2_b[l]))
    return _post(x, ya, yb, g1, sh2, sc2, g2, post_w, POST_TILE)
```

```python
import math

import jax
import jax.numpy as jnp
from jax import lax
from jax.experimental import pallas as pl
from jax.experimental.pallas import tpu as pltpu

F32 = jnp.float32
BF16 = jnp.bfloat16

D_MODEL = 1024
D_GMLP = 512
G_A = 4
CHUNK = 128
D_MLSTM = 512
H_B = 4
HV = 128
HK = 64
D_QK = 256
D_FF = 4096
DEPTH = 1
ALPHA = (2 * DEPTH) ** 0.25
LN_EPS = 1e-5
LOG2E = math.log2(math.e)

HALO = 16
SUB = 8
N_GATE = 4
Q_C, Q_CM, Q_B, Q_CMAX, Q_BTOT, Q_M0, N_GQ = 0, 2, 4, 6, 8, 10, 12
AUG = 16
HVA = HV + AUG
VMEM_LIMIT_BYTES = 56 * 1024 * 1024
PROJ_TILE = 1024
PROJ_SUB = 256
POST_TILE = 1024
POST_SUB = 256
POST_FF = 1024
OUT_GROUP = 8

NT_DIMS = (((1,), (1,)), ((), ()))


def _ln_stats(x):
    mu = jnp.mean(x, axis=-1, keepdims=True)
    xc = x - mu
    var = jnp.mean(xc * xc, axis=-1, keepdims=True)
    return xc * lax.rsqrt(var + LN_EPS)


def _sigmoid(x):
    return 1.0 / (1.0 + jnp.exp(-x))


def _const_spec(shape):
    nd = len(shape)
    return pl.BlockSpec(shape, lambda *_: (0,) * nd, pipeline_mode=pl.Buffered(1))


def _ada_kernel(c_ref, w_ref, b_ref, o_ref):
    c = c_ref[...]
    s = (c * _sigmoid(c)).astype(BF16)
    o_ref[...] = jnp.dot(s, w_ref[...].astype(BF16), preferred_element_type=F32) + b_ref[...]


def _ada(cond, w, b):
    m, d = cond.shape
    n = w.shape[1]
    bn = 1536
    return pl.pallas_call(
        _ada_kernel,
        grid=(n // bn,),
        in_specs=[pl.BlockSpec((m, d), lambda j: (0, 0)),
                  pl.BlockSpec((d, bn), lambda j: (0, j)),
                  pl.BlockSpec((1, bn), lambda j: (0, j))],
        out_specs=pl.BlockSpec((m, bn), lambda j: (0, j)),
        out_shape=jax.ShapeDtypeStruct((m, n), F32),
        name="ada",
    )(cond, w, b)


def _proj_kernel(x_ref, xp_ref, xn_ref, sh_ref, sc_ref, wuv_ref, wqk_ref, wvt_ref, wo_ref,
                 ws_ref, bs_ref, lnvg_ref, lnvb_ref, cw_ref, qs_ref, bg_ref, hng_ref,
                 ya_ref, kk_ref, qt_ref, vt_ref, og_ref, gr_ref):
    i = pl.program_id(1)
    nt = pl.num_programs(1)
    tm = x_ref.shape[0]
    ts = min(PROJ_SUB, tm)
    nsub = tm // ts
    sc1 = 1.0 + sc_ref[...]
    sh = sh_ref[...]
    cw = cw_ref[...]
    lane = lax.broadcasted_iota(jnp.int32, (ts, CHUNK), 1)

    def mod(xv):
        return _ln_stats(xv) * sc1 + sh

    def project(s, hm, hext):
        r0 = s * ts
        rows = slice(r0, r0 + ts)
        puv = jnp.dot(hm, wuv_ref[...], preferred_element_type=F32)
        pqk = jnp.dot(hext, wqk_ref[...], preferred_element_type=F32)
        vtg = lax.dot_general(wvt_ref[...], hm, NT_DIMS, preferred_element_type=F32)
        vt_ref[:, rows] = vtg[:D_MLSTM, :].astype(BF16)
        og_ref[rows, :] = (_sigmoid(jnp.dot(hm, wo_ref[...], preferred_element_type=F32)) * hng_ref[...]).astype(BF16)

        conv = (pltpu.roll(pqk, 1, 0) * cw[0:1, :] + pqk * cw[1:2, :]
                + pltpu.roll(pqk, ts + 2 * HALO - 1, 0) * cw[2:3, :])
        conv = conv[HALO:HALO + ts, :]
        act = conv * _sigmoid(conv) * qs_ref[...]
        for pair in range(H_B // 2):
            blk = act[:, D_QK + pair * CHUNK:D_QK + (pair + 1) * CHUNK]
            swp = pltpu.roll(blk, HK, 1)
            kk_ref[rows, (2 * pair) * CHUNK:(2 * pair + 1) * CHUNK] = jnp.where(lane < HK, blk, swp).astype(BF16)
            kk_ref[rows, (2 * pair + 1) * CHUNK:(2 * pair + 2) * CHUNK] = jnp.where(lane < HK, swp, blk).astype(BF16)
            for c in range(ts // CHUNK):
                blk = act[c * CHUNK:(c + 1) * CHUNK, pair * CHUNK:(pair + 1) * CHUNK]
                qt_ref[pair * CHUNK:(pair + 1) * CHUNK, r0 + c * CHUNK:r0 + (c + 1) * CHUNK] = blk.T.astype(BF16)

        gt = vtg[D_MLSTM:, :] + bg_ref[...]
        for h in range(H_B):
            for q in range(N_GATE):
                for c in range(ts // CHUNK):
                    jr = r0 // CHUNK + c
                    gr_ref[h, q, jr:jr + 1, :] = gt[h * N_GATE + q:h * N_GATE + q + 1, c * CHUNK:(c + 1) * CHUNK]

        lnv = (_ln_stats(puv[:, D_GMLP:]) * lnvg_ref[...] + lnvb_ref[...]).astype(BF16)
        nch = ts // CHUNK
        for g in range(G_A):
            cols = slice(g * CHUNK, (g + 1) * CHUNK)
            side = jnp.concatenate([lnv[c * CHUNK:(c + 1) * CHUNK, cols] for c in range(nch)], axis=1)
            mixed = jnp.dot(ws_ref[g], side, preferred_element_type=F32)
            for c in range(nch):
                m_c = mixed[:, c * CHUNK:(c + 1) * CHUNK] + bs_ref[g]
                ya_ref[r0 + c * CHUNK:r0 + (c + 1) * CHUNK, cols] = (
                    puv[c * CHUNK:(c + 1) * CHUNK, cols] * m_c).astype(BF16)

    if gr_ref.shape[2] > tm // CHUNK:
        gr_ref[:, :, tm // CHUNK:, :] = jnp.zeros((H_B, N_GATE, gr_ref.shape[2] - tm // CHUNK, CHUNK), F32)
    keep_prev = jnp.where(i > 0, 1.0, 0.0)
    keep_next = jnp.where(i < nt - 1, 1.0, 0.0)
    left = (mod(xp_ref[...]) * keep_prev).astype(BF16)
    cur = mod(x_ref[0:ts, :]).astype(BF16)
    for s in range(nsub):
        if s + 1 < nsub:
            nxt = mod(x_ref[(s + 1) * ts:(s + 2) * ts, :]).astype(BF16)
        else:
            nxt = (mod(xn_ref[...]) * keep_next).astype(BF16)
        project(s, cur, jnp.concatenate([left, cur, nxt[:HALO, :]], axis=0))
        left, cur = cur[ts - HALO:, :], nxt


def _proj(x, sh, sc, wts, tm):
    bsz, s, d = x.shape
    nt = s // tm
    hb = tm // HALO
    nhb = s // HALO
    (wuv, wqk, wvt, wo, ws, bs, lnvg, lnvb, cw, qs, bg, hng) = wts
    bg_t = jnp.broadcast_to(bg, (bg.shape[0], min(PROJ_SUB, tm)))
    tok = lambda n: jax.ShapeDtypeStruct((bsz, s, n), BF16)
    tok_spec = lambda n: pl.BlockSpec((None, tm, n), lambda b, i: (b, i, 0))
    tr = lambda n: jax.ShapeDtypeStruct((bsz, n, s), BF16)
    tr_spec = lambda n: pl.BlockSpec((None, n, tm), lambda b, i: (b, 0, i))
    vec_spec = pl.BlockSpec((None, 1, d), lambda b, i: (b, 0, 0))
    return pl.pallas_call(
        _proj_kernel,
        grid=(bsz, nt),
        in_specs=[
            pl.BlockSpec((None, tm, d), lambda b, i: (b, i, 0)),
            pl.BlockSpec((None, HALO, d), lambda b, i: (b, jnp.maximum(i * hb - 1, 0), 0)),
            pl.BlockSpec((None, HALO, d), lambda b, i: (b, jnp.minimum((i + 1) * hb, nhb - 1), 0)),
            vec_spec, vec_spec,
            _const_spec(wuv.shape), _const_spec(wqk.shape), _const_spec(wvt.shape), _const_spec(wo.shape),
            _const_spec(ws.shape), _const_spec(bs.shape), _const_spec(lnvg.shape), _const_spec(lnvb.shape),
            _const_spec(cw.shape), _const_spec(qs.shape), _const_spec(bg_t.shape), _const_spec(hng.shape),
        ],
        out_specs=[tok_spec(D_GMLP), tok_spec(2 * D_QK), tr_spec(D_QK), tr_spec(D_MLSTM), tok_spec(D_MLSTM),
                   pl.BlockSpec((None, H_B, N_GATE, max(tm // CHUNK, SUB), CHUNK), lambda b, i: (b, 0, 0, i, 0))],
        out_shape=[tok(D_GMLP), tok(2 * D_QK), tr(D_QK), tr(D_MLSTM), tok(D_MLSTM),
                   jax.ShapeDtypeStruct((bsz, H_B, N_GATE, max(s // CHUNK, SUB), CHUNK), F32)],
        compiler_params=pltpu.CompilerParams(
            dimension_semantics=("parallel", "arbitrary"), vmem_limit_bytes=VMEM_LIMIT_BYTES),
        name="proj",
    )(x, x, x, sh, sc, wuv, wqk, wvt, wo, ws, bs, lnvg, lnvb, cw, qs, bg_t, hng)


def _prefix(x, op, lane, fill):
    d = 1
    while d < CHUNK:
        x = op(x, jnp.where(lane >= d, pltpu.roll(x, d, 1), fill))
        d *= 2
    return x


def _suffix(x, op, lane, fill):
    d = 1
    while d < CHUNK:
        x = op(x, jnp.where(lane < CHUNK - d, pltpu.roll(x, CHUNK - d, 1), fill))
        d *= 2
    return x


def _mlstm_kernel(kk_ref, qt_ref, vt_ref, og_ref, gr_ref, kkc_ref, vtc_ref, grc_ref, yb_ref,
                  gq_s, dc_s, stf_s, stb_s, rhs_s):
    ncx = kkc_ref.shape[0] // CHUNK
    ncl = kk_ref.shape[0] // CHUNK
    lat0 = grc_ref.shape[1]
    aug_rows = jnp.where(lax.broadcasted_iota(jnp.int32, (AUG, CHUNK), 0) == 0, 1.0, 0.0).astype(BF16)
    row = lax.broadcasted_iota(jnp.int32, (SUB, CHUNK), 0)
    lane1 = lax.broadcasted_iota(jnp.int32, (1, CHUNK), 1)
    lane_st = lax.broadcasted_iota(jnp.int32, (HVA, CHUNK), 1)

    def vec(q, j):
        return gq_s[q, pl.ds(j, 1), :]

    gates = [jnp.concatenate([grc_ref[q], gr_ref[q]], axis=0) for q in range(N_GATE)]
    lane = lax.broadcasted_iota(jnp.int32, gates[0].shape, 1)
    scans = ((_prefix, 0), (_suffix, 1))
    for scan, r in scans:
        gf = gates[2 + r]
        lf = (jnp.minimum(gf, 0.0) - jnp.log(1.0 + jnp.exp(-jnp.abs(gf)))) * LOG2E
        b = scan(lf, jnp.add, lane, 0.0)
        c = gates[r] * LOG2E - b
        gq_s[Q_C + r] = c
        gq_s[Q_CM + r] = scan(c, jnp.maximum, lane, -jnp.inf)
        gq_s[Q_B + r] = b
        gq_s[Q_CMAX + r] = jnp.broadcast_to(jnp.max(c, axis=1, keepdims=True), c.shape)
        gq_s[Q_BTOT + r] = jnp.broadcast_to(jnp.min(b, axis=1, keepdims=True), c.shape)

    def delta_state(kk, vt, j):
        vta = jnp.concatenate([vt, aug_rows], axis=0)
        lhs = jnp.concatenate([vta * jnp.exp2(vec(Q_C + r, j) - vec(Q_CMAX + r, j)).astype(BF16)
                               for r in range(2)], axis=0)
        d2 = jnp.dot(lhs, kk, preferred_element_type=F32)
        dc_s[j] = jnp.where(lane_st < HK, d2[:HVA, :], d2[HVA:, :])

    for jc in range(ncx):
        sl = slice(jc * CHUNK, (jc + 1) * CHUNK)
        delta_state(kkc_ref[sl, :], vtc_ref[:, sl], jc)

    def prep_body(jl, carry):
        t0 = pl.multiple_of(jl * CHUNK, CHUNK)
        delta_state(kk_ref[pl.ds(t0, CHUNK), :], vt_ref[:, pl.ds(t0, CHUNK)], jl + lat0)
        return carry

    lax.fori_loop(0, ncl, prep_body, 0, unroll=4)

    def scan_step(js, state, ms):
        stf_s[js[0]] = state.astype(BF16)
        stb_s[js[1]] = state.astype(BF16)
        coef, new_ms = [], []
        for r in range(2):
            gq_s[Q_M0 + r, pl.ds(js[r], 1), :] = ms[r]
            cmax = vec(Q_CMAX + r, js[r])
            g = jnp.maximum(ms[r], cmax)
            coef.append((jnp.exp2(ms[r] - g), jnp.exp2(cmax - g)))
            new_ms.append(vec(Q_BTOT + r, js[r]) + g)
        a_l = jnp.where(lane1 < HK, coef[0][0], coef[1][0])
        b_l = jnp.where(lane1 < HK, coef[0][1], coef[1][1])
        dc = jnp.where(lane_st < HK, dc_s[js[0]], dc_s[js[1]])
        return a_l * state + b_l * dc, tuple(new_ms)

    state = jnp.zeros((HVA, CHUNK), F32)
    ms = (jnp.zeros((1, CHUNK), F32), jnp.zeros((1, CHUNK), F32))
    for p in range(ncx):
        state, ms = scan_step((p, ncx - 1 - p), state, ms)

    def scan_body(p, carry):
        return scan_step((lat0 + p, lat0 + ncl - 1 - p), *carry)

    lax.fori_loop(0, ncl, scan_body, (state, ms), unroll=2)

    si = lax.broadcasted_iota(jnp.int32, (CHUNK, CHUNK), 0)
    ti = lax.broadcasted_iota(jnp.int32, (CHUNK, CHUNK), 1)
    masks = (si <= ti, si >= ti)
    zpad = jnp.zeros((CHUNK - SUB, CHUNK), F32)
    zq = jnp.zeros((HK, CHUNK), BF16)

    def stabiliser(r, j):
        m0 = vec(Q_M0 + r, j)
        return m0, jnp.maximum(m0, vec(Q_CM + r, j))

    def weights_stage(jl):
        j = jl + lat0
        t0 = pl.multiple_of(jl * CHUNK, CHUNK)
        c2 = jnp.where(row == 0, vec(Q_C, j), jnp.where(row == 1, vec(Q_C + 1, j), 0.0))
        ct = jnp.concatenate([c2, zpad], axis=0).T
        qt = qt_ref[:, pl.ds(t0, CHUNK)]
        st = lax.dot_general(kk_ref[pl.ds(t0, CHUNK), :], jnp.concatenate([qt, zq], axis=0),
                             (((1,), (0,)), ((), ())), preferred_element_type=F32)
        qtf = qt.astype(F32)
        for r in range(2):
            m0, g = stabiliser(r, j)
            w = jnp.where(masks[r], jnp.exp2(ct[:, r:r + 1] - g), 0.0)
            rhs_s[jl, 0:CHUNK, r * CHUNK:(r + 1) * CHUNK] = (st * w).astype(BF16)
            qw = (qtf * jnp.exp2(m0 - g)).astype(BF16)
            rhs_s[jl, CHUNK + r * HK:CHUNK + (r + 1) * HK, r * CHUNK:(r + 1) * CHUNK] = qw
            rhs_s[jl, CHUNK + (1 - r) * HK:CHUNK + (2 - r) * HK, r * CHUNK:(r + 1) * CHUNK] = zq

    def output_stage(jl):
        j = jl + lat0
        t0 = pl.multiple_of(jl * CHUNK, CHUNK)
        vta = jnp.concatenate([vt_ref[:, pl.ds(t0, CHUNK)], aug_rows], axis=0)
        stt = jnp.where(lane_st < HK, stf_s[j], stb_s[j])
        nd = jnp.dot(jnp.concatenate([vta, stt], axis=1), rhs_s[jl], preferred_element_type=F32)
        ht = None
        for r in range(2):
            blk = nd[:, r * CHUNK:(r + 1) * CHUNK]
            fl = jnp.exp2(-(vec(Q_B + r, j) + stabiliser(r, j)[1]))
            rden = 1.0 / jnp.maximum(jnp.abs(blk[HV:HV + 1, :]), fl)
            part = blk[:HV, :] * rden
            ht = part if ht is None else ht + part
        mu = jnp.mean(ht, axis=0, keepdims=True)
        xc = ht - mu
        var = jnp.mean(xc * xc, axis=0, keepdims=True)
        y = (xc * lax.rsqrt(var + LN_EPS)).T
        yb_ref[pl.ds(t0, CHUNK), :] = y.astype(BF16) * og_ref[pl.ds(t0, CHUNK), :]

    for u in range(OUT_GROUP):
        weights_stage(u)

    def out_body(i, carry):
        for u in range(OUT_GROUP):
            output_stage(i * OUT_GROUP + u)
        for u in range(OUT_GROUP):
            weights_stage((i + 1) * OUT_GROUP + u)
        return carry

    lax.fori_loop(0, ncl // OUT_GROUP - 1, out_body, 0)
    for u in range(OUT_GROUP):
        output_stage(ncl - OUT_GROUP + u)


def _mlstm(kk, qt, vt, og, gr, kkc, vtc, grc):
    bsz, s, _ = kk.shape
    sc = kkc.shape[1]
    ncl = s // CHUNK
    nr = grc.shape[3] + gr.shape[3]
    head = lambda n: pl.BlockSpec((None, n, CHUNK), lambda b, h: (b, 0, h))
    head_t = lambda rows, n: pl.BlockSpec((None, rows, n), lambda b, h: (b, h, 0))
    gates = lambda a: pl.BlockSpec((None, None, N_GATE, a.shape[3], CHUNK), lambda b, h: (b, h, 0, 0, 0))
    return pl.pallas_call(
        _mlstm_kernel,
        grid=(bsz, H_B),
        in_specs=[head(s), head_t(HK, s), head_t(HV, s), head(s), gates(gr),
                  head(sc), head_t(HV, sc), gates(grc)],
        out_specs=head(s),
        out_shape=jax.ShapeDtypeStruct((bsz, s, D_MLSTM), BF16),
        scratch_shapes=[pltpu.VMEM((N_GQ, nr, CHUNK), F32),
                        pltpu.VMEM((nr, HVA, CHUNK), F32),
                        pltpu.VMEM((nr, HVA, CHUNK), BF16), pltpu.VMEM((nr, HVA, CHUNK), BF16),
                        pltpu.VMEM((ncl, 2 * CHUNK, 2 * CHUNK), BF16)],
        compiler_params=pltpu.CompilerParams(
            dimension_semantics=("parallel", "arbitrary"), vmem_limit_bytes=VMEM_LIMIT_BYTES),
        name="mlstm",
    )(kk, qt, vt, og, gr, kkc, vtc, grc)


def _post_kernel(x_ref, ya_ref, yb_ref, g1_ref, sh2_ref, sc2_ref, g2_ref, wo_ref, l1g_ref, l1b_ref,
                 w1_ref, b1_ref, w2_ref, b2_ref, l2g_ref, l2b_ref, out_ref):
    nf = POST_FF
    nparts = D_FF // nf
    bounds = list(range(0, x_ref.shape[0] + 1, POST_SUB))
    nsub = len(bounds) - 1

    def head(r):
        rows = slice(bounds[r], bounds[r + 1])
        yab = jnp.concatenate([ya_ref[rows, :], yb_ref[rows, :]], axis=1)
        y = jnp.dot(yab, wo_ref[...], preferred_element_type=F32)
        x1 = _ln_stats(ALPHA * x_ref[rows, :] + g1_ref[...] * y) * l1g_ref[...] + l1b_ref[...]
        h2 = (_ln_stats(x1) * (1.0 + sc2_ref[...]) + sh2_ref[...]).astype(BF16)
        return x1, h2

    def up(h2, f):
        cols = slice(f * nf, (f + 1) * nf)
        return jnp.dot(h2, w1_ref[:, cols], preferred_element_type=F32) + b1_ref[:, cols]

    def down(t, f):
        t = jnp.maximum(t, 0.0)
        return jnp.dot((t * t).astype(BF16), w2_ref[f * nf:(f + 1) * nf, :], preferred_element_type=F32)

    def tail(r, x1, z):
        rows = slice(bounds[r], bounds[r + 1])
        out_ref[rows, :] = _ln_stats(ALPHA * x1 + g2_ref[...] * (z + b2_ref[...])) * l2g_ref[...] + l2b_ref[...]

    cur = head(0)
    pending = None
    for r in range(nsub):
        x1, h2 = cur
        t = up(h2, 0)
        z = None
        for f in range(nparts):
            t_next = up(h2, f + 1) if f + 1 < nparts else None
            if f == min(1, nparts - 1) and pending is not None:
                tail(*pending)
            if f == min(2, nparts - 1) and r + 1 < nsub:
                cur = head(r + 1)
            d = down(t, f)
            z = d if z is None else z + d
            t = t_next
        pending = (r, x1, z)
    tail(*pending)


def _post(x, ya, yb, g1, sh2, sc2, g2, wts, tm):
    bsz, s, d = x.shape
    (wo, l1g, l1b, w1, b1, w2, b2, l2g, l2b) = wts
    vec_spec = pl.BlockSpec((None, 1, d), lambda b, i: (b, 0, 0))
    tok_spec = lambda n: pl.BlockSpec((None, tm, n), lambda b, i: (b, i, 0))
    return pl.pallas_call(
        _post_kernel,
        grid=(bsz, s // tm),
        in_specs=[tok_spec(d), tok_spec(D_GMLP), tok_spec(D_MLSTM), vec_spec, vec_spec, vec_spec, vec_spec]
                 + [_const_spec(w.shape) for w in wts],
        out_specs=tok_spec(d),
        out_shape=jax.ShapeDtypeStruct((bsz, s, d), F32),
        compiler_params=pltpu.CompilerParams(
            dimension_semantics=("parallel", "arbitrary"), vmem_limit_bytes=VMEM_LIMIT_BYTES),
        name="post",
    )(x, ya, yb, g1, sh2, sc2, g2, wo, l1g, l1b, w1, b1, w2, b2, l2g, l2b)


def _proj_weights(w_in, w_s, b_s, ln_v_g, ln_v_b, conv_qk, b_gates, hn_g):
    q0 = 2 * D_GMLP
    v0 = q0 + 2 * D_QK
    o0 = v0 + D_MLSTM
    g0 = o0 + D_MLSTM
    def gate_tiles(a):
        return a.reshape(2, 2, H_B, -1).transpose(2, 1, 0, 3).reshape(H_B * N_GATE, -1)

    wuv = w_in[:, :q0].astype(BF16)
    wqk = w_in[:, q0:v0].astype(BF16)
    wvt = jnp.concatenate([w_in[:, v0:o0].T, gate_tiles(w_in[:, g0:].T)], axis=0).astype(BF16)
    wo = w_in[:, o0:g0].astype(BF16)
    ws = w_s.astype(BF16)
    bs = jnp.broadcast_to(b_s[:, :, None], (G_A, CHUNK, CHUNK))
    qs = jnp.concatenate([jnp.full((D_QK,), HK ** -0.5, F32), jnp.ones((D_QK,), F32)])[None, :]
    return (wuv, wqk, wvt, wo, ws, bs, ln_v_g[None, :], ln_v_b[None, :], conv_qk, qs, gate_tiles(b_gates[:, None]),
            hn_g[None, :])


def kernel(x, c, ctx, c_ctx, w_ada, b_ada, w_in, w_s, b_s, ln_v_g, ln_v_b, conv_qk, b_gates, hn_g,
           w_out, ln1_g, ln1_b, w1, b1, w2, b2, ln2_g, ln2_b):
    bsz = x.shape[0]
    l = 0
    cond = jnp.concatenate([c, c_ctx[None, :], jnp.zeros((16 - bsz - 1, D_MODEL), F32)], axis=0)
    mods = _ada(cond, w_ada[l], b_ada[l][None, :])
    lat = [mods[:bsz, None, i * D_MODEL:(i + 1) * D_MODEL] for i in range(6)]
    cx = [jnp.broadcast_to(mods[bsz, i * D_MODEL:(i + 1) * D_MODEL], (bsz, 1, D_MODEL)) for i in range(2)]
    sh1, sc1, g1, sh2, sc2, g2 = lat

    pw = _proj_weights(w_in[l], w_s[l], b_s[l], ln_v_g[l], ln_v_b[l], conv_qk[l], b_gates[l], hn_g[l])
    ya, kk, qt, vt, og, gr = _proj(x, sh1, sc1, pw, PROJ_TILE)
    _, kkc, _, vtc, _, grc = _proj(ctx, cx[0], cx[1], pw, ctx.shape[1])

    yb = _mlstm(kk, qt, vt, og, gr, kkc, vtc, grc)

    row = lambda v: v[None, :]
    post_w = (w_out[l].astype(BF16), row(ln1_g[l]), row(ln1_b[l]), w1[l].astype(BF16), row(b1[l]),
              w2[l].astype(BF16), row(b2[l]), row(ln2_g[l]), row(ln2_b[l]))
    return _post(x, ya, yb, g1, sh2, sc2, g2, post_w, POST_TILE)
```

```python
import math

import jax
import jax.numpy as jnp
from jax import lax
from jax.experimental import pallas as pl
from jax.experimental.pallas import tpu as pltpu

F32 = jnp.float32
BF16 = jnp.bfloat16

D_MODEL = 1024
D_GMLP = 512
G_A = 4
CHUNK = 128
D_MLSTM = 512
H_B = 4
HV = 128
HK = 64
D_QK = 256
D_FF = 4096
DEPTH = 1
ALPHA = (2 * DEPTH) ** 0.25
LN_EPS = 1e-5
LOG2E = math.log2(math.e)

ADA_ROWS = 16
HALO = 16
SUB = 8
N_GATE = 4
Q_C, Q_CM, Q_B, Q_CMAX, Q_BTOT, N_GQ = 0, 2, 4, 6, 8, 10
AUG = 16
HVA = HV + AUG
VMEM_LIMIT_BYTES = 56 * 1024 * 1024
PROJ_TILE = 1024
PROJ_SUB = 256
POST_TILE = 1024
POST_SUB = 256
POST_FF = 1024
OUT_GROUP = 8

NT_DIMS = (((1,), (1,)), ((), ()))


def _ln_stats(x):
    mu = jnp.mean(x, axis=-1, keepdims=True)
    xc = x - mu
    var = jnp.mean(xc * xc, axis=-1, keepdims=True)
    return xc * lax.rsqrt(var + LN_EPS)


def _sigmoid(x):
    return 1.0 / (1.0 + jnp.exp(-x))


def _const_spec(shape):
    nd = len(shape)
    return pl.BlockSpec(shape, lambda *_: (0,) * nd, pipeline_mode=pl.Buffered(1))


def _ada_kernel(c_ref, w_ref, b_ref, o_ref):
    c = c_ref[...]
    s = (c * _sigmoid(c)).astype(BF16)
    o_ref[...] = jnp.dot(s, w_ref[...].astype(BF16), preferred_element_type=F32) + b_ref[...]


def _ada(cond, w, b):
    m, d = cond.shape
    n = w.shape[1]
    bn = 1536
    return pl.pallas_call(
        _ada_kernel,
        grid=(n // bn,),
        in_specs=[pl.BlockSpec((m, d), lambda j: (0, 0)),
                  pl.BlockSpec((d, bn), lambda j: (0, j)),
                  pl.BlockSpec((1, bn), lambda j: (0, j))],
        out_specs=pl.BlockSpec((m, bn), lambda j: (0, j)),
        out_shape=jax.ShapeDtypeStruct((m, n), F32),
        name="ada",
    )(cond, w, b)


def _proj_kernel(x_ref, xp_ref, xn_ref, sh_ref, sc_ref, wuv_ref, wqk_ref, wvt_ref, wo_ref,
                 ws_ref, bs_ref, lnvg_ref, lnvb_ref, cw_ref, qs_ref, bg_ref, hng_ref,
                 ya_ref, kk_ref, qt_ref, vt_ref, og_ref, gr_ref):
    i = pl.program_id(1)
    nt = pl.num_programs(1)
    tm = x_ref.shape[0]
    ts = min(PROJ_SUB, tm)
    nsub = tm // ts
    sc1 = 1.0 + sc_ref[...]
    sh = sh_ref[...]
    cw = cw_ref[...]
    lane = lax.broadcasted_iota(jnp.int32, (ts, CHUNK), 1)

    def mod(xv):
        return _ln_stats(xv) * sc1 + sh

    def project(s, hm, hext):
        r0 = s * ts
        rows = slice(r0, r0 + ts)
        puv = jnp.dot(hm, wuv_ref[...], preferred_element_type=F32)
        pqk = jnp.dot(hext, wqk_ref[...], preferred_element_type=F32)
        vtg = lax.dot_general(wvt_ref[...], hm, NT_DIMS, preferred_element_type=F32)
        vt_ref[:, rows] = vtg[:D_MLSTM, :].astype(BF16)
        og_ref[rows, :] = (_sigmoid(jnp.dot(hm, wo_ref[...], preferred_element_type=F32)) * hng_ref[...]).astype(BF16)

        conv = (pltpu.roll(pqk, 1, 0) * cw[0:1, :] + pqk * cw[1:2, :]
                + pltpu.roll(pqk, ts + 2 * HALO - 1, 0) * cw[2:3, :])
        conv = conv[HALO:HALO + ts, :]
        act = conv * _sigmoid(conv) * qs_ref[...]
        for pair in range(H_B // 2):
            blk = act[:, D_QK + pair * CHUNK:D_QK + (pair + 1) * CHUNK]
            swp = pltpu.roll(blk, HK, 1)
            kk_ref[rows, (2 * pair) * CHUNK:(2 * pair + 1) * CHUNK] = jnp.where(lane < HK, blk, swp).astype(BF16)
            kk_ref[rows, (2 * pair + 1) * CHUNK:(2 * pair + 2) * CHUNK] = jnp.where(lane < HK, swp, blk).astype(BF16)
            for c in range(ts // CHUNK):
                blk = act[c * CHUNK:(c + 1) * CHUNK, pair * CHUNK:(pair + 1) * CHUNK]
                qt_ref[pair * CHUNK:(pair + 1) * CHUNK, r0 + c * CHUNK:r0 + (c + 1) * CHUNK] = blk.T.astype(BF16)

        gt = vtg[D_MLSTM:, :] + bg_ref[...]
        for h in range(H_B):
            for q in range(N_GATE):
                for c in range(ts // CHUNK):
                    jr = r0 // CHUNK + c
                    gr_ref[h, q, jr:jr + 1, :] = gt[h * N_GATE + q:h * N_GATE + q + 1, c * CHUNK:(c + 1) * CHUNK]

        lnv = (_ln_stats(puv[:, D_GMLP:]) * lnvg_ref[...] + lnvb_ref[...]).astype(BF16)
        nch = ts // CHUNK
        for g in range(G_A):
            cols = slice(g * CHUNK, (g + 1) * CHUNK)
            side = jnp.concatenate([lnv[c * CHUNK:(c + 1) * CHUNK, cols] for c in range(nch)], axis=1)
            mixed = jnp.dot(ws_ref[g], side, preferred_element_type=F32)
            for c in range(nch):
                m_c = mixed[:, c * CHUNK:(c + 1) * CHUNK] + bs_ref[g]
                ya_ref[r0 + c * CHUNK:r0 + (c + 1) * CHUNK, cols] = (
                    puv[c * CHUNK:(c + 1) * CHUNK, cols] * m_c).astype(BF16)

    if gr_ref.shape[2] > tm // CHUNK:
        gr_ref[:, :, tm // CHUNK:, :] = jnp.zeros((H_B, N_GATE, gr_ref.shape[2] - tm // CHUNK, CHUNK), F32)
    keep_prev = jnp.where(i > 0, 1.0, 0.0)
    keep_next = jnp.where(i < nt - 1, 1.0, 0.0)
    left = (mod(xp_ref[...]) * keep_prev).astype(BF16)
    cur = mod(x_ref[0:ts, :]).astype(BF16)
    for s in range(nsub):
        if s + 1 < nsub:
            nxt = mod(x_ref[(s + 1) * ts:(s + 2) * ts, :]).astype(BF16)
        else:
            nxt = (mod(xn_ref[...]) * keep_next).astype(BF16)
        project(s, cur, jnp.concatenate([left, cur, nxt[:HALO, :]], axis=0))
        left, cur = cur[ts - HALO:, :], nxt


def _mod_spec(k, row=None):
    return pl.BlockSpec((None, 1, D_MODEL), lambda b, i: (b if row is None else row, 0, k))


def _proj(x, mods, mod_row, wts, tm):
    bsz, s, d = x.shape
    nt = s // tm
    hb = tm // HALO
    nhb = s // HALO
    (wuv, wqk, wvt, wo, ws, bs, lnvg, lnvb, cw, qs, bg, hng) = wts
    bg_t = jnp.broadcast_to(bg, (bg.shape[0], min(PROJ_SUB, tm)))
    tok = lambda n: jax.ShapeDtypeStruct((bsz, s, n), BF16)
    tok_spec = lambda n: pl.BlockSpec((None, tm, n), lambda b, i: (b, i, 0))
    tr = lambda n: jax.ShapeDtypeStruct((bsz, n, s), BF16)
    tr_spec = lambda n: pl.BlockSpec((None, n, tm), lambda b, i: (b, 0, i))
    return pl.pallas_call(
        _proj_kernel,
        grid=(bsz, nt),
        in_specs=[
            pl.BlockSpec((None, tm, d), lambda b, i: (b, i, 0)),
            pl.BlockSpec((None, HALO, d), lambda b, i: (b, jnp.maximum(i * hb - 1, 0), 0)),
            pl.BlockSpec((None, HALO, d), lambda b, i: (b, jnp.minimum((i + 1) * hb, nhb - 1), 0)),
            _mod_spec(0, mod_row), _mod_spec(1, mod_row),
            _const_spec(wuv.shape), _const_spec(wqk.shape), _const_spec(wvt.shape), _const_spec(wo.shape),
            _const_spec(ws.shape), _const_spec(bs.shape), _const_spec(lnvg.shape), _const_spec(lnvb.shape),
            _const_spec(cw.shape), _const_spec(qs.shape), _const_spec(bg_t.shape), _const_spec(hng.shape),
        ],
        out_specs=[tok_spec(D_GMLP), tok_spec(2 * D_QK), tr_spec(D_QK), tr_spec(D_MLSTM), tok_spec(D_MLSTM),
                   pl.BlockSpec((None, H_B, N_GATE, max(tm // CHUNK, SUB), CHUNK), lambda b, i: (b, 0, 0, i, 0))],
        out_shape=[tok(D_GMLP), tok(2 * D_QK), tr(D_QK), tr(D_MLSTM), tok(D_MLSTM),
                   jax.ShapeDtypeStruct((bsz, H_B, N_GATE, max(s // CHUNK, SUB), CHUNK), F32)],
        compiler_params=pltpu.CompilerParams(
            dimension_semantics=("parallel", "arbitrary"), vmem_limit_bytes=VMEM_LIMIT_BYTES),
        name="proj",
    )(x, x, x, mods, mods, wuv, wqk, wvt, wo, ws, bs, lnvg, lnvb, cw, qs, bg_t, hng)


def _prefix(x, op, lane, fill):
    d = 1
    while d < CHUNK:
        x = op(x, jnp.where(lane >= d, pltpu.roll(x, d, 1), fill))
        d *= 2
    return x


def _suffix(x, op, lane, fill):
    d = 1
    while d < CHUNK:
        x = op(x, jnp.where(lane < CHUNK - d, pltpu.roll(x, CHUNK - d, 1), fill))
        d *= 2
    return x


def _gates_kernel(gr_ref, grc_ref, gq_ref, gqc_ref):
    for src, dst in ((gr_ref, gq_ref), (grc_ref, gqc_ref)):
        nrow = src.shape[2]
        shape = (H_B * nrow, CHUNK)
        lane = lax.broadcasted_iota(jnp.int32, shape, 1)
        for r, scan in enumerate((_prefix, _suffix)):
            gi = src[:, r, :, :].reshape(shape)
            gf = src[:, 2 + r, :, :].reshape(shape)
            lf = (jnp.minimum(gf, 0.0) - jnp.log(1.0 + jnp.exp(-jnp.abs(gf)))) * LOG2E
            b = scan(lf, jnp.add, lane, 0.0)
            c = gi * LOG2E - b
            planes = ((Q_C, c), (Q_CM, scan(c, jnp.maximum, lane, -jnp.inf)), (Q_B, b),
                      (Q_CMAX, jnp.broadcast_to(jnp.max(c, axis=1, keepdims=True), shape)),
                      (Q_BTOT, jnp.broadcast_to(jnp.min(b, axis=1, keepdims=True), shape)))
            for q, val in planes:
                dst[:, q + r, :, :] = val.reshape(H_B, nrow, CHUNK)


def _gates(gr, grc):
    bsz = gr.shape[0]
    spec = lambda a, n: pl.BlockSpec((None, H_B, n, a.shape[3], CHUNK), lambda b: (b, 0, 0, 0, 0))
    shape = lambda a: jax.ShapeDtypeStruct((bsz, H_B, N_GQ, a.shape[3], CHUNK), F32)
    return pl.pallas_call(
        _gates_kernel,
        grid=(bsz,),
        in_specs=[spec(gr, N_GATE), spec(grc, N_GATE)],
        out_specs=[spec(gr, N_GQ), spec(grc, N_GQ)],
        out_shape=[shape(gr), shape(grc)],
        name="gates",
    )(gr, grc)


def _mlstm_kernel(kk_ref, qt_ref, vt_ref, og_ref, gr_ref, kkc_ref, vtc_ref, grc_ref, yb_ref,
                  m0_s, dc_s, stf_s, stb_s, rhs_s):
    ncx = kkc_ref.shape[0] // CHUNK
    ncl = kk_ref.shape[0] // CHUNK
    aug_rows = jnp.where(lax.broadcasted_iota(jnp.int32, (AUG, CHUNK), 0) == 0, 1.0, 0.0).astype(BF16)
    row = lax.broadcasted_iota(jnp.int32, (SUB, CHUNK), 0)
    lane1 = lax.broadcasted_iota(jnp.int32, (1, CHUNK), 1)
    lane_st = lax.broadcasted_iota(jnp.int32, (HVA, CHUNK), 1)

    def ctx_chunk(jc):
        return jc, (lambda q: grc_ref[q, jc:jc + 1, :]), None

    def lat_chunk(jl):
        return ncx + jl, (lambda q: gr_ref[q, pl.ds(jl, 1), :]), jl

    def delta_state(kk, vt, chunk):
        j, vec, _ = chunk
        vta = jnp.concatenate([vt, aug_rows], axis=0)
        lhs = jnp.concatenate([vta * jnp.exp2(vec(Q_C + r) - vec(Q_CMAX + r)).astype(BF16)
                               for r in range(2)], axis=0)
        d2 = jnp.dot(lhs, kk, preferred_element_type=F32)
        dc_s[j] = jnp.where(lane_st < HK, d2[:HVA, :], d2[HVA:, :])

    for jc in range(ncx):
        sl = slice(jc * CHUNK, (jc + 1) * CHUNK)
        delta_state(kkc_ref[sl, :], vtc_ref[:, sl], ctx_chunk(jc))

    def prep_body(jl, carry):
        t0 = pl.multiple_of(jl * CHUNK, CHUNK)
        delta_state(kk_ref[pl.ds(t0, CHUNK), :], vt_ref[:, pl.ds(t0, CHUNK)], lat_chunk(jl))
        return carry

    lax.fori_loop(0, ncl, prep_body, 0, unroll=4)

    def scan_step(chunks, state, ms):
        stf_s[chunks[0][0]] = state.astype(BF16)
        stb_s[chunks[1][0]] = state.astype(BF16)
        coef, new_ms = [], []
        for r, (_, vec, jl) in enumerate(chunks):
            if jl is not None:
                m0_s[r, pl.ds(jl, 1), :] = ms[r]
            cmax = vec(Q_CMAX + r)
            g = jnp.maximum(ms[r], cmax)
            coef.append((jnp.exp2(ms[r] - g), jnp.exp2(cmax - g)))
            new_ms.append(vec(Q_BTOT + r) + g)
        a_l = jnp.where(lane1 < HK, coef[0][0], coef[1][0])
        b_l = jnp.where(lane1 < HK, coef[0][1], coef[1][1])
        dc = jnp.where(lane_st < HK, dc_s[chunks[0][0]], dc_s[chunks[1][0]])
        return a_l * state + b_l * dc, tuple(new_ms)

    state = jnp.zeros((HVA, CHUNK), F32)
    ms = (jnp.zeros((1, CHUNK), F32), jnp.zeros((1, CHUNK), F32))
    for p in range(ncx):
        state, ms = scan_step((ctx_chunk(p), ctx_chunk(ncx - 1 - p)), state, ms)

    def scan_body(p, carry):
        return scan_step((lat_chunk(p), lat_chunk(ncl - 1 - p)), *carry)

    lax.fori_loop(0, ncl, scan_body, (state, ms), unroll=2)

    si = lax.broadcasted_iota(jnp.int32, (CHUNK, CHUNK), 0)
    ti = lax.broadcasted_iota(jnp.int32, (CHUNK, CHUNK), 1)
    masks = (si <= ti, si >= ti)
    zpad = jnp.zeros((CHUNK - SUB, CHUNK), F32)
    zq = jnp.zeros((HK, CHUNK), BF16)

    def stabiliser(r, jl):
        m0 = m0_s[r, pl.ds(jl, 1), :]
        return m0, jnp.maximum(m0, gr_ref[Q_CM + r, pl.ds(jl, 1), :])

    def weights_stage(jl):
        t0 = pl.multiple_of(jl * CHUNK, CHUNK)
        vec = lat_chunk(jl)[1]
        c2 = jnp.where(row == 0, vec(Q_C), jnp.where(row == 1, vec(Q_C + 1), 0.0))
        ct = jnp.concatenate([c2, zpad], axis=0).T
        qt = qt_ref[:, pl.ds(t0, CHUNK)]
        st = lax.dot_general(kk_ref[pl.ds(t0, CHUNK), :], jnp.concatenate([qt, zq], axis=0),
                             (((1,), (0,)), ((), ())), preferred_element_type=F32)
        qtf = qt.astype(F32)
        for r in range(2):
            m0, g = stabiliser(r, jl)
            w = jnp.where(masks[r], jnp.exp2(ct[:, r:r + 1] - g), 0.0)
            rhs_s[jl, 0:CHUNK, r * CHUNK:(r + 1) * CHUNK] = (st * w).astype(BF16)
            qw = (qtf * jnp.exp2(m0 - g)).astype(BF16)
            rhs_s[jl, CHUNK + r * HK:CHUNK + (r + 1) * HK, r * CHUNK:(r + 1) * CHUNK] = qw
            rhs_s[jl, CHUNK + (1 - r) * HK:CHUNK + (2 - r) * HK, r * CHUNK:(r + 1) * CHUNK] = zq

    def output_stage(jl):
        j = ncx + jl
        t0 = pl.multiple_of(jl * CHUNK, CHUNK)
        vta = jnp.concatenate([vt_ref[:, pl.ds(t0, CHUNK)], aug_rows], axis=0)
        stt = jnp.where(lane_st < HK, stf_s[j], stb_s[j])
        nd = jnp.dot(jnp.concatenate([vta, stt], axis=1), rhs_s[jl], preferred_element_type=F32)
        ht = None
        for r in range(2):
            blk = nd[:, r * CHUNK:(r + 1) * CHUNK]
            fl = jnp.exp2(-(gr_ref[Q_B + r, pl.ds(jl, 1), :] + stabiliser(r, jl)[1]))
            rden = 1.0 / jnp.maximum(jnp.abs(blk[HV:HV + 1, :]), fl)
            part = blk[:HV, :] * rden
            ht = part if ht is None else ht + part
        mu = jnp.mean(ht, axis=0, keepdims=True)
        xc = ht - mu
        var = jnp.mean(xc * xc, axis=0, keepdims=True)
        y = (xc * lax.rsqrt(var + LN_EPS)).T
        yb_ref[pl.ds(t0, CHUNK), :] = y.astype(BF16) * og_ref[pl.ds(t0, CHUNK), :]

    for u in range(OUT_GROUP):
        weights_stage(u)

    def out_body(i, carry):
        for u in range(OUT_GROUP):
            output_stage(i * OUT_GROUP + u)
        for u in range(OUT_GROUP):
            weights_stage((i + 1) * OUT_GROUP + u)
        return carry

    lax.fori_loop(0, ncl // OUT_GROUP - 1, out_body, 0)
    for u in range(OUT_GROUP):
        output_stage(ncl - OUT_GROUP + u)


def _mlstm(kk, qt, vt, og, gr, kkc, vtc, grc):
    bsz, s, _ = kk.shape
    sc = kkc.shape[1]
    ncl = s // CHUNK
    nr = sc // CHUNK + ncl
    head = lambda n: pl.BlockSpec((None, n, CHUNK), lambda b, h: (b, 0, h))
    head_t = lambda rows, n: pl.BlockSpec((None, rows, n), lambda b, h: (b, h, 0))
    gates = lambda a: pl.BlockSpec((None, None, N_GQ, a.shape[3], CHUNK), lambda b, h: (b, h, 0, 0, 0))
    return pl.pallas_call(
        _mlstm_kernel,
        grid=(bsz, H_B),
        in_specs=[head(s), head_t(HK, s), head_t(HV, s), head(s), gates(gr),
                  head(sc), head_t(HV, sc), gates(grc)],
        out_specs=head(s),
        out_shape=jax.ShapeDtypeStruct((bsz, s, D_MLSTM), BF16),
        scratch_shapes=[pltpu.VMEM((2, ncl, CHUNK), F32),
                        pltpu.VMEM((nr, HVA, CHUNK), F32),
                        pltpu.VMEM((nr, HVA, CHUNK), BF16), pltpu.VMEM((nr, HVA, CHUNK), BF16),
                        pltpu.VMEM((ncl, 2 * CHUNK, 2 * CHUNK), BF16)],
        compiler_params=pltpu.CompilerParams(
            dimension_semantics=("parallel", "arbitrary"), vmem_limit_bytes=VMEM_LIMIT_BYTES),
        name="mlstm",
    )(kk, qt, vt, og, gr, kkc, vtc, grc)


def _post_kernel(x_ref, ya_ref, yb_ref, g1_ref, sh2_ref, sc2_ref, g2_ref, wo_ref, l1g_ref, l1b_ref,
                 w1_ref, b1_ref, w2_ref, b2_ref, l2g_ref, l2b_ref, out_ref):
    nf = POST_FF
    nparts = D_FF // nf
    bounds = list(range(0, x_ref.shape[0] + 1, POST_SUB))
    nsub = len(bounds) - 1

    def head(r):
        rows = slice(bounds[r], bounds[r + 1])
        yab = jnp.concatenate([ya_ref[rows, :], yb_ref[rows, :]], axis=1)
        y = jnp.dot(yab, wo_ref[...], preferred_element_type=F32)
        x1 = _ln_stats(ALPHA * x_ref[rows, :] + g1_ref[...] * y) * l1g_ref[...] + l1b_ref[...]
        h2 = (_ln_stats(x1) * (1.0 + sc2_ref[...]) + sh2_ref[...]).astype(BF16)
        return x1, h2

    def up(h2, f):
        cols = slice(f * nf, (f + 1) * nf)
        return jnp.dot(h2, w1_ref[:, cols], preferred_element_type=F32) + b1_ref[:, cols]

    def down(t, f):
        t = jnp.maximum(t, 0.0)
        return jnp.dot((t * t).astype(BF16), w2_ref[f * nf:(f + 1) * nf, :], preferred_element_type=F32)

    def tail(r, x1, z):
        rows = slice(bounds[r], bounds[r + 1])
        out_ref[rows, :] = _ln_stats(ALPHA * x1 + g2_ref[...] * (z + b2_ref[...])) * l2g_ref[...] + l2b_ref[...]

    cur = head(0)
    pending = None
    for r in range(nsub):
        x1, h2 = cur
        t = up(h2, 0)
        z = None
        for f in range(nparts):
            t_next = up(h2, f + 1) if f + 1 < nparts else None
            if f == min(1, nparts - 1) and pending is not None:
                tail(*pending)
            if f == min(2, nparts - 1) and r + 1 < nsub:
                cur = head(r + 1)
            d = down(t, f)
            z = d if z is None else z + d
            t = t_next
        pending = (r, x1, z)
    tail(*pending)


def _post(x, ya, yb, mods, wts, tm):
    bsz, s, d = x.shape
    (wo, l1g, l1b, w1, b1, w2, b2, l2g, l2b) = wts
    tok_spec = lambda n: pl.BlockSpec((None, tm, n), lambda b, i: (b, i, 0))
    return pl.pallas_call(
        _post_kernel,
        grid=(bsz, s // tm),
        in_specs=[tok_spec(d), tok_spec(D_GMLP), tok_spec(D_MLSTM),
                  _mod_spec(2), _mod_spec(3), _mod_spec(4), _mod_spec(5)]
                 + [_const_spec(w.shape) for w in wts],
        out_specs=tok_spec(d),
        out_shape=jax.ShapeDtypeStruct((bsz, s, d), F32),
        compiler_params=pltpu.CompilerParams(
            dimension_semantics=("parallel", "arbitrary"), vmem_limit_bytes=VMEM_LIMIT_BYTES),
        name="post",
    )(x, ya, yb, mods, mods, mods, mods, wo, l1g, l1b, w1, b1, w2, b2, l2g, l2b)


def _proj_weights(w_in, w_s, b_s, ln_v_g, ln_v_b, conv_qk, b_gates, hn_g):
    q0 = 2 * D_GMLP
    v0 = q0 + 2 * D_QK
    o0 = v0 + D_MLSTM
    g0 = o0 + D_MLSTM
    def gate_tiles(a):
        return a.reshape(2, 2, H_B, -1).transpose(2, 1, 0, 3).reshape(H_B * N_GATE, -1)

    wuv = w_in[:, :q0].astype(BF16)
    wqk = w_in[:, q0:v0].astype(BF16)
    wvt = jnp.concatenate([w_in[:, v0:o0].T, gate_tiles(w_in[:, g0:].T)], axis=0).astype(BF16)
    wo = w_in[:, o0:g0].astype(BF16)
    ws = w_s.astype(BF16)
    bs = jnp.broadcast_to(b_s[:, :, None], (G_A, CHUNK, CHUNK))
    qs = jnp.concatenate([jnp.full((D_QK,), HK ** -0.5, F32), jnp.ones((D_QK,), F32)])[None, :]
    return (wuv, wqk, wvt, wo, ws, bs, ln_v_g[None, :], ln_v_b[None, :], conv_qk, qs, gate_tiles(b_gates[:, None]),
            hn_g[None, :])


def kernel(x, c, ctx, c_ctx, w_ada, b_ada, w_in, w_s, b_s, ln_v_g, ln_v_b, conv_qk, b_gates, hn_g,
           w_out, ln1_g, ln1_b, w1, b1, w2, b2, ln2_g, ln2_b):
    bsz = x.shape[0]
    l = 0
    cond = jnp.concatenate([c, c_ctx[None, :], jnp.zeros((ADA_ROWS - bsz - 1, D_MODEL), F32)], axis=0)
    mods = _ada(cond, w_ada[l], b_ada[l][None, :])[:, None, :]

    pw = _proj_weights(w_in[l], w_s[l], b_s[l], ln_v_g[l], ln_v_b[l], conv_qk[l], b_gates[l], hn_g[l])
    ya, kk, qt, vt, og, gr = _proj(x, mods, None, pw, PROJ_TILE)
    _, kkc, _, vtc, _, grc = _proj(ctx, mods, bsz, pw, ctx.shape[1])

    gq, gqc = _gates(gr, grc)
    yb = _mlstm(kk, qt, vt, og, gq, kkc, vtc, gqc)

    row = lambda v: v[None, :]
    post_w = (w_out[l].astype(BF16), row(ln1_g[l]), row(ln1_b[l]), w1[l].astype(BF16), row(b1[l]),
              w2[l].astype(BF16), row(b2[l]), row(ln2_g[l]), row(ln2_b[l]))
    return _post(x, ya, yb, mods, post_w, POST_TILE)
```

```python
import math

import jax
import jax.numpy as jnp
from jax import lax
from jax.experimental import pallas as pl
from jax.experimental.pallas import tpu as pltpu

F32 = jnp.float32
BF16 = jnp.bfloat16

D_MODEL = 1024
D_GMLP = 512
G_A = 4
CHUNK = 128
D_MLSTM = 512
H_B = 4
HV = 128
HK = 64
D_QK = 256
D_FF = 4096
DEPTH = 1
ALPHA = (2 * DEPTH) ** 0.25
LN_EPS = 1e-5
LOG2E = math.log2(math.e)

ADA_ROWS = 16
HALO = 16
SUB = 8
N_GATE = 4
Q_C, Q_CM, Q_B, Q_CMAX, Q_BTOT, N_GQ = 0, 2, 4, 6, 8, 10
AUG = 16
HVA = HV + AUG
VMEM_LIMIT_BYTES = 56 * 1024 * 1024
PROJ_TILE = 1024
PROJ_SUB = 256
POST_TILE = 1024
POST_SUB = 256
POST_FF = 1024
OUT_GROUP = 8

NT_DIMS = (((1,), (1,)), ((), ()))


def _ln_stats(x):
    mu = jnp.mean(x, axis=-1, keepdims=True)
    xc = x - mu
    var = jnp.mean(xc * xc, axis=-1, keepdims=True)
    return xc * lax.rsqrt(var + LN_EPS)


def _sigmoid(x):
    return 1.0 / (1.0 + jnp.exp(-x))


def _const_spec(shape):
    nd = len(shape)
    return pl.BlockSpec(shape, lambda *_: (0,) * nd, pipeline_mode=pl.Buffered(1))


def _ada_kernel(c_ref, w_ref, b_ref, o_ref):
    c = c_ref[...]
    s = (c * _sigmoid(c)).astype(BF16)
    o_ref[...] = jnp.dot(s, w_ref[...].astype(BF16), preferred_element_type=F32) + b_ref[...]


def _ada(cond, w, b):
    m, d = cond.shape
    n = w.shape[1]
    bn = 1536
    return pl.pallas_call(
        _ada_kernel,
        grid=(n // bn,),
        in_specs=[pl.BlockSpec((m, d), lambda j: (0, 0)),
                  pl.BlockSpec((d, bn), lambda j: (0, j)),
                  pl.BlockSpec((1, bn), lambda j: (0, j))],
        out_specs=pl.BlockSpec((m, bn), lambda j: (0, j)),
        out_shape=jax.ShapeDtypeStruct((m, n), F32),
        name="ada",
    )(cond, w, b)


def _proj_kernel(x_ref, xp_ref, xn_ref, sh_ref, sc_ref, wuv_ref, wqk_ref, wvt_ref, wo_ref,
                 ws_ref, bs_ref, lnvg_ref, lnvb_ref, cw_ref, qs_ref, bg_ref, hng_ref,
                 ya_ref, kk_ref, qt_ref, vt_ref, og_ref, gr_ref):
    i = pl.program_id(1)
    nt = pl.num_programs(1)
    tm = x_ref.shape[0]
    ts = min(PROJ_SUB, tm)
    nsub = tm // ts
    sc1 = 1.0 + sc_ref[...]
    sh = sh_ref[...]
    cw = cw_ref[...]
    lane = lax.broadcasted_iota(jnp.int32, (ts, CHUNK), 1)

    def mod(xv):
        return _ln_stats(xv) * sc1 + sh

    def project(s, hm, hext):
        r0 = s * ts
        rows = slice(r0, r0 + ts)
        puv = jnp.dot(hm, wuv_ref[...], preferred_element_type=F32)
        pqk = jnp.dot(hext, wqk_ref[...], preferred_element_type=F32)
        vtg = lax.dot_general(wvt_ref[...], hm, NT_DIMS, preferred_element_type=F32)
        vt_ref[:, rows] = vtg[:D_MLSTM, :].astype(BF16)
        og_ref[rows, :] = (_sigmoid(jnp.dot(hm, wo_ref[...], preferred_element_type=F32)) * hng_ref[...]).astype(BF16)

        conv = (pltpu.roll(pqk, 1, 0) * cw[0:1, :] + pqk * cw[1:2, :]
                + pltpu.roll(pqk, ts + 2 * HALO - 1, 0) * cw[2:3, :])
        conv = conv[HALO:HALO + ts, :]
        act = conv * _sigmoid(conv) * qs_ref[...]
        for pair in range(H_B // 2):
            blk = act[:, D_QK + pair * CHUNK:D_QK + (pair + 1) * CHUNK]
            swp = pltpu.roll(blk, HK, 1)
            kk_ref[rows, (2 * pair) * CHUNK:(2 * pair + 1) * CHUNK] = jnp.where(lane < HK, blk, swp).astype(BF16)
            kk_ref[rows, (2 * pair + 1) * CHUNK:(2 * pair + 2) * CHUNK] = jnp.where(lane < HK, swp, blk).astype(BF16)
            for c in range(ts // CHUNK):
                blk = act[c * CHUNK:(c + 1) * CHUNK, pair * CHUNK:(pair + 1) * CHUNK]
                qt_ref[pair * CHUNK:(pair + 1) * CHUNK, r0 + c * CHUNK:r0 + (c + 1) * CHUNK] = blk.T.astype(BF16)

        gt = vtg[D_MLSTM:, :] + bg_ref[...]
        for h in range(H_B):
            for q in range(N_GATE):
                for c in range(ts // CHUNK):
                    jr = r0 // CHUNK + c
                    gr_ref[h, q, jr:jr + 1, :] = gt[h * N_GATE + q:h * N_GATE + q + 1, c * CHUNK:(c + 1) * CHUNK]

        lnv = (_ln_stats(puv[:, D_GMLP:]) * lnvg_ref[...] + lnvb_ref[...]).astype(BF16)
        nch = ts // CHUNK
        for g in range(G_A):
            cols = slice(g * CHUNK, (g + 1) * CHUNK)
            side = jnp.concatenate([lnv[c * CHUNK:(c + 1) * CHUNK, cols] for c in range(nch)], axis=1)
            mixed = jnp.dot(ws_ref[g], side, preferred_element_type=F32)
            for c in range(nch):
                m_c = mixed[:, c * CHUNK:(c + 1) * CHUNK] + bs_ref[g]
                ya_ref[r0 + c * CHUNK:r0 + (c + 1) * CHUNK, cols] = (
                    puv[c * CHUNK:(c + 1) * CHUNK, cols] * m_c).astype(BF16)

    if gr_ref.shape[2] > tm // CHUNK:
        gr_ref[:, :, tm // CHUNK:, :] = jnp.zeros((H_B, N_GATE, gr_ref.shape[2] - tm // CHUNK, CHUNK), F32)
    keep_prev = jnp.where(i > 0, 1.0, 0.0)
    keep_next = jnp.where(i < nt - 1, 1.0, 0.0)
    left = (mod(xp_ref[...]) * keep_prev).astype(BF16)
    cur = mod(x_ref[0:ts, :]).astype(BF16)
    for s in range(nsub):
        if s + 1 < nsub:
            nxt = mod(x_ref[(s + 1) * ts:(s + 2) * ts, :]).astype(BF16)
        else:
            nxt = (mod(xn_ref[...]) * keep_next).astype(BF16)
        project(s, cur, jnp.concatenate([left, cur, nxt[:HALO, :]], axis=0))
        left, cur = cur[ts - HALO:, :], nxt


def _mod_spec(k, row=None):
    return pl.BlockSpec((None, 1, D_MODEL), lambda b, i: (b if row is None else row, 0, k))


def _proj(x, mods, mod_row, wts, tm):
    bsz, s, d = x.shape
    nt = s // tm
    hb = tm // HALO
    nhb = s // HALO
    (wuv, wqk, wvt, wo, ws, bs, lnvg, lnvb, cw, qs, bg, hng) = wts
    bg_t = jnp.broadcast_to(bg, (bg.shape[0], min(PROJ_SUB, tm)))
    tok = lambda n: jax.ShapeDtypeStruct((bsz, s, n), BF16)
    tok_spec = lambda n: pl.BlockSpec((None, tm, n), lambda b, i: (b, i, 0))
    tr = lambda n: jax.ShapeDtypeStruct((bsz, n, s), BF16)
    tr_spec = lambda n: pl.BlockSpec((None, n, tm), lambda b, i: (b, 0, i))
    return pl.pallas_call(
        _proj_kernel,
        grid=(bsz, nt),
        in_specs=[
            pl.BlockSpec((None, tm, d), lambda b, i: (b, i, 0)),
            pl.BlockSpec((None, HALO, d), lambda b, i: (b, jnp.maximum(i * hb - 1, 0), 0)),
            pl.BlockSpec((None, HALO, d), lambda b, i: (b, jnp.minimum((i + 1) * hb, nhb - 1), 0)),
            _mod_spec(0, mod_row), _mod_spec(1, mod_row),
            _const_spec(wuv.shape), _const_spec(wqk.shape), _const_spec(wvt.shape), _const_spec(wo.shape),
            _const_spec(ws.shape), _const_spec(bs.shape), _const_spec(lnvg.shape), _const_spec(lnvb.shape),
            _const_spec(cw.shape), _const_spec(qs.shape), _const_spec(bg_t.shape), _const_spec(hng.shape),
        ],
        out_specs=[tok_spec(D_GMLP), tok_spec(2 * D_QK), tr_spec(D_QK), tr_spec(D_MLSTM), tok_spec(D_MLSTM),
                   pl.BlockSpec((None, H_B, N_GATE, max(tm // CHUNK, SUB), CHUNK), lambda b, i: (b, 0, 0, i, 0))],
        out_shape=[tok(D_GMLP), tok(2 * D_QK), tr(D_QK), tr(D_MLSTM), tok(D_MLSTM),
                   jax.ShapeDtypeStruct((bsz, H_B, N_GATE, max(s // CHUNK, SUB), CHUNK), F32)],
        compiler_params=pltpu.CompilerParams(
            dimension_semantics=("parallel", "arbitrary"), vmem_limit_bytes=VMEM_LIMIT_BYTES),
        name="proj",
    )(x, x, x, mods, mods, wuv, wqk, wvt, wo, ws, bs, lnvg, lnvb, cw, qs, bg_t, hng)


def _prefix(x, op, lane, fill):
    d = 1
    while d < CHUNK:
        x = op(x, jnp.where(lane >= d, pltpu.roll(x, d, 1), fill))
        d *= 2
    return x


def _suffix(x, op, lane, fill):
    d = 1
    while d < CHUNK:
        x = op(x, jnp.where(lane < CHUNK - d, pltpu.roll(x, CHUNK - d, 1), fill))
        d *= 2
    return x


def _gates_kernel(gr_ref, grc_ref, gq_ref, gqc_ref):
    for src, dst in ((gr_ref, gq_ref), (grc_ref, gqc_ref)):
        nrow = src.shape[2]
        shape = (H_B * nrow, CHUNK)
        lane = lax.broadcasted_iota(jnp.int32, shape, 1)
        for r, scan in enumerate((_prefix, _suffix)):
            gi = src[:, r, :, :].reshape(shape)
            gf = src[:, 2 + r, :, :].reshape(shape)
            lf = (jnp.minimum(gf, 0.0) - jnp.log(1.0 + jnp.exp(-jnp.abs(gf)))) * LOG2E
            b = scan(lf, jnp.add, lane, 0.0)
            c = gi * LOG2E - b
            planes = ((Q_C, c), (Q_CM, scan(c, jnp.maximum, lane, -jnp.inf)), (Q_B, b),
                      (Q_CMAX, jnp.broadcast_to(jnp.max(c, axis=1, keepdims=True), shape)),
                      (Q_BTOT, jnp.broadcast_to(jnp.min(b, axis=1, keepdims=True), shape)))
            for q, val in planes:
                dst[:, q + r, :, :] = val.reshape(H_B, nrow, CHUNK)


def _gates(gr, grc):
    bsz = gr.shape[0]
    spec = lambda a, n: pl.BlockSpec((None, H_B, n, a.shape[3], CHUNK), lambda b: (b, 0, 0, 0, 0))
    shape = lambda a: jax.ShapeDtypeStruct((bsz, H_B, N_GQ, a.shape[3], CHUNK), F32)
    return pl.pallas_call(
        _gates_kernel,
        grid=(bsz,),
        in_specs=[spec(gr, N_GATE), spec(grc, N_GATE)],
        out_specs=[spec(gr, N_GQ), spec(grc, N_GQ)],
        out_shape=[shape(gr), shape(grc)],
        name="gates",
    )(gr, grc)


def _mlstm_kernel(kk_ref, qt_ref, vt_ref, og_ref, gr_ref, kkc_ref, vtc_ref, grc_ref, yb_ref,
                  m0_s, dc_s, stf_s, stb_s, rhs_s):
    ncx = kkc_ref.shape[0] // CHUNK
    ncl = kk_ref.shape[0] // CHUNK
    aug_rows = jnp.where(lax.broadcasted_iota(jnp.int32, (AUG, CHUNK), 0) == 0, 1.0, 0.0).astype(BF16)
    row = lax.broadcasted_iota(jnp.int32, (SUB, CHUNK), 0)
    lane1 = lax.broadcasted_iota(jnp.int32, (1, CHUNK), 1)
    lane_st = lax.broadcasted_iota(jnp.int32, (HVA, CHUNK), 1)

    def ctx_chunk(jc):
        return jc, (lambda q: grc_ref[q, jc:jc + 1, :]), None

    def lat_chunk(jl):
        return ncx + jl, (lambda q: gr_ref[q, pl.ds(jl, 1), :]), jl

    def delta_state(kk, vt, chunk):
        j, vec, _ = chunk
        vta = jnp.concatenate([vt, aug_rows], axis=0)
        lhs = jnp.concatenate([vta * jnp.exp2(vec(Q_C + r) - vec(Q_CMAX + r)).astype(BF16)
                               for r in range(2)], axis=0)
        d2 = jnp.dot(lhs, kk, preferred_element_type=F32)
        dc_s[j] = jnp.where(lane_st < HK, d2[:HVA, :], d2[HVA:, :])

    for jc in range(ncx):
        sl = slice(jc * CHUNK, (jc + 1) * CHUNK)
        delta_state(kkc_ref[sl, :], vtc_ref[:, sl], ctx_chunk(jc))

    def prep_body(jl, carry):
        t0 = pl.multiple_of(jl * CHUNK, CHUNK)
        delta_state(kk_ref[pl.ds(t0, CHUNK), :], vt_ref[:, pl.ds(t0, CHUNK)], lat_chunk(jl))
        return carry

    lax.fori_loop(0, ncl, prep_body, 0, unroll=8)

    def scan_step(chunks, state, ms):
        stf_s[chunks[0][0]] = state.astype(BF16)
        stb_s[chunks[1][0]] = state.astype(BF16)
        coef, new_ms = [], []
        for r, (_, vec, jl) in enumerate(chunks):
            if jl is not None:
                m0_s[r, pl.ds(jl, 1), :] = ms[r]
            cmax = vec(Q_CMAX + r)
            g = jnp.maximum(ms[r], cmax)
            coef.append((jnp.exp2(ms[r] - g), jnp.exp2(cmax - g)))
            new_ms.append(vec(Q_BTOT + r) + g)
        a_l = jnp.where(lane1 < HK, coef[0][0], coef[1][0])
        b_l = jnp.where(lane1 < HK, coef[0][1], coef[1][1])
        dc = jnp.where(lane_st < HK, dc_s[chunks[0][0]], dc_s[chunks[1][0]])
        return a_l * state + b_l * dc, tuple(new_ms)

    state = jnp.zeros((HVA, CHUNK), F32)
    ms = (jnp.zeros((1, CHUNK), F32), jnp.zeros((1, CHUNK), F32))
    for p in range(ncx):
        state, ms = scan_step((ctx_chunk(p), ctx_chunk(ncx - 1 - p)), state, ms)

    def scan_body(p, carry):
        return scan_step((lat_chunk(p), lat_chunk(ncl - 1 - p)), *carry)

    lax.fori_loop(0, ncl, scan_body, (state, ms), unroll=2)

    si = lax.broadcasted_iota(jnp.int32, (CHUNK, CHUNK), 0)
    ti = lax.broadcasted_iota(jnp.int32, (CHUNK, CHUNK), 1)
    masks = (si <= ti, si >= ti)
    zpad = jnp.zeros((CHUNK - SUB, CHUNK), F32)
    zq = jnp.zeros((HK, CHUNK), BF16)

    def stabiliser(r, jl):
        m0 = m0_s[r, pl.ds(jl, 1), :]
        return m0, jnp.maximum(m0, gr_ref[Q_CM + r, pl.ds(jl, 1), :])

    def weights_stage(jl):
        t0 = pl.multiple_of(jl * CHUNK, CHUNK)
        vec = lat_chunk(jl)[1]
        c2 = jnp.where(row == 0, vec(Q_C), jnp.where(row == 1, vec(Q_C + 1), 0.0))
        ct = jnp.concatenate([c2, zpad], axis=0).T
        qt = qt_ref[:, pl.ds(t0, CHUNK)]
        st = lax.dot_general(kk_ref[pl.ds(t0, CHUNK), :], jnp.concatenate([qt, zq], axis=0),
                             (((1,), (0,)), ((), ())), preferred_element_type=F32)
        qtf = qt.astype(F32)
        for r in range(2):
            m0, g = stabiliser(r, jl)
            w = jnp.where(masks[r], jnp.exp2(ct[:, r:r + 1] - g), 0.0)
            rhs_s[jl, 0:CHUNK, r * CHUNK:(r + 1) * CHUNK] = (st * w).astype(BF16)
            qw = (qtf * jnp.exp2(m0 - g)).astype(BF16)
            rhs_s[jl, CHUNK + r * HK:CHUNK + (r + 1) * HK, r * CHUNK:(r + 1) * CHUNK] = qw
            rhs_s[jl, CHUNK + (1 - r) * HK:CHUNK + (2 - r) * HK, r * CHUNK:(r + 1) * CHUNK] = zq

    def output_stage(jl):
        j = ncx + jl
        t0 = pl.multiple_of(jl * CHUNK, CHUNK)
        vta = jnp.concatenate([vt_ref[:, pl.ds(t0, CHUNK)], aug_rows], axis=0)
        stt = jnp.where(lane_st < HK, stf_s[j], stb_s[j])
        nd = jnp.dot(jnp.concatenate([vta, stt], axis=1), rhs_s[jl], preferred_element_type=F32)
        ht = None
        for r in range(2):
            blk = nd[:, r * CHUNK:(r + 1) * CHUNK]
            fl = jnp.exp2(-(gr_ref[Q_B + r, pl.ds(jl, 1), :] + stabiliser(r, jl)[1]))
            rden = 1.0 / jnp.maximum(jnp.abs(blk[HV:HV + 1, :]), fl)
            part = blk[:HV, :] * rden
            ht = part if ht is None else ht + part
        mu = jnp.mean(ht, axis=0, keepdims=True)
        xc = ht - mu
        var = jnp.mean(xc * xc, axis=0, keepdims=True)
        y = (xc * lax.rsqrt(var + LN_EPS)).T
        yb_ref[pl.ds(t0, CHUNK), :] = y.astype(BF16) * og_ref[pl.ds(t0, CHUNK), :]

    for u in range(OUT_GROUP):
        weights_stage(u)

    def out_body(i, carry):
        for u in range(OUT_GROUP):
            output_stage(i * OUT_GROUP + u)
        for u in range(OUT_GROUP):
            weights_stage((i + 1) * OUT_GROUP + u)
        return carry

    lax.fori_loop(0, ncl // OUT_GROUP - 1, out_body, 0)
    for u in range(OUT_GROUP):
        output_stage(ncl - OUT_GROUP + u)


def _mlstm(kk, qt, vt, og, gr, kkc, vtc, grc):
    bsz, s, _ = kk.shape
    sc = kkc.shape[1]
    ncl = s // CHUNK
    nr = sc // CHUNK + ncl
    head = lambda n: pl.BlockSpec((None, n, CHUNK), lambda b, h: (b, 0, h))
    head_t = lambda rows, n: pl.BlockSpec((None, rows, n), lambda b, h: (b, h, 0))
    gates = lambda a: pl.BlockSpec((None, None, N_GQ, a.shape[3], CHUNK), lambda b, h: (b, h, 0, 0, 0))
    return pl.pallas_call(
        _mlstm_kernel,
        grid=(bsz, H_B),
        in_specs=[head(s), head_t(HK, s), head_t(HV, s), head(s), gates(gr),
                  head(sc), head_t(HV, sc), gates(grc)],
        out_specs=head(s),
        out_shape=jax.ShapeDtypeStruct((bsz, s, D_MLSTM), BF16),
        scratch_shapes=[pltpu.VMEM((2, ncl, CHUNK), F32),
                        pltpu.VMEM((nr, HVA, CHUNK), F32),
                        pltpu.VMEM((nr, HVA, CHUNK), BF16), pltpu.VMEM((nr, HVA, CHUNK), BF16),
                        pltpu.VMEM((ncl, 2 * CHUNK, 2 * CHUNK), BF16)],
        compiler_params=pltpu.CompilerParams(
            dimension_semantics=("parallel", "arbitrary"), vmem_limit_bytes=VMEM_LIMIT_BYTES),
        name="mlstm",
    )(kk, qt, vt, og, gr, kkc, vtc, grc)


def _post_kernel(x_ref, ya_ref, yb_ref, g1_ref, sh2_ref, sc2_ref, g2_ref, wo_ref, l1g_ref, l1b_ref,
                 w1_ref, b1_ref, w2_ref, b2_ref, l2g_ref, l2b_ref, out_ref):
    nf = POST_FF
    nparts = D_FF // nf
    bounds = list(range(0, x_ref.shape[0] + 1, POST_SUB))
    nsub = len(bounds) - 1

    def head(r):
        rows = slice(bounds[r], bounds[r + 1])
        yab = jnp.concatenate([ya_ref[rows, :], yb_ref[rows, :]], axis=1)
        y = jnp.dot(yab, wo_ref[...], preferred_element_type=F32)
        x1 = _ln_stats(ALPHA * x_ref[rows, :] + g1_ref[...] * y) * l1g_ref[...] + l1b_ref[...]
        h2 = (_ln_stats(x1) * (1.0 + sc2_ref[...]) + sh2_ref[...]).astype(BF16)
        return x1, h2

    def up(h2, f):
        cols = slice(f * nf, (f + 1) * nf)
        return jnp.dot(h2, w1_ref[:, cols], preferred_element_type=F32) + b1_ref[:, cols]

    def down(t, f):
        t = jnp.maximum(t, 0.0)
        return jnp.dot((t * t).astype(BF16), w2_ref[f * nf:(f + 1) * nf, :], preferred_element_type=F32)

    def tail(r, x1, z):
        rows = slice(bounds[r], bounds[r + 1])
        out_ref[rows, :] = _ln_stats(ALPHA * x1 + g2_ref[...] * (z + b2_ref[...])) * l2g_ref[...] + l2b_ref[...]

    cur = head(0)
    pending = None
    for r in range(nsub):
        x1, h2 = cur
        t = up(h2, 0)
        z = None
        for f in range(nparts):
            t_next = up(h2, f + 1) if f + 1 < nparts else None
            if f == min(1, nparts - 1) and pending is not None:
                tail(*pending)
            if f == min(2, nparts - 1) and r + 1 < nsub:
                cur = head(r + 1)
            d = down(t, f)
            z = d if z is None else z + d
            t = t_next
        pending = (r, x1, z)
    tail(*pending)


def _post(x, ya, yb, mods, wts, tm):
    bsz, s, d = x.shape
    (wo, l1g, l1b, w1, b1, w2, b2, l2g, l2b) = wts
    tok_spec = lambda n: pl.BlockSpec((None, tm, n), lambda b, i: (b, i, 0))
    return pl.pallas_call(
        _post_kernel,
        grid=(bsz, s // tm),
        in_specs=[tok_spec(d), tok_spec(D_GMLP), tok_spec(D_MLSTM),
                  _mod_spec(2), _mod_spec(3), _mod_spec(4), _mod_spec(5)]
                 + [_const_spec(w.shape) for w in wts],
        out_specs=tok_spec(d),
        out_shape=jax.ShapeDtypeStruct((bsz, s, d), F32),
        compiler_params=pltpu.CompilerParams(
            dimension_semantics=("parallel", "arbitrary"), vmem_limit_bytes=VMEM_LIMIT_BYTES),
        name="post",
    )(x, ya, yb, mods, mods, mods, mods, wo, l1g, l1b, w1, b1, w2, b2, l2g, l2b)


def _proj_weights(w_in, w_s, b_s, ln_v_g, ln_v_b, conv_qk, b_gates, hn_g):
    q0 = 2 * D_GMLP
    v0 = q0 + 2 * D_QK
    o0 = v0 + D_MLSTM
    g0 = o0 + D_MLSTM
    def gate_tiles(a):
        return a.reshape(2, 2, H_B, -1).transpose(2, 1, 0, 3).reshape(H_B * N_GATE, -1)

    wuv = w_in[:, :q0].astype(BF16)
    wqk = w_in[:, q0:v0].astype(BF16)
    wvt = jnp.concatenate([w_in[:, v0:o0].T, gate_tiles(w_in[:, g0:].T)], axis=0).astype(BF16)
    wo = w_in[:, o0:g0].astype(BF16)
    ws = w_s.astype(BF16)
    bs = jnp.broadcast_to(b_s[:, :, None], (G_A, CHUNK, CHUNK))
    qs = jnp.concatenate([jnp.full((D_QK,), HK ** -0.5, F32), jnp.ones((D_QK,), F32)])[None, :]
    return (wuv, wqk, wvt, wo, ws, bs, ln_v_g[None, :], ln_v_b[None, :], conv_qk, qs, gate_tiles(b_gates[:, None]),
            hn_g[None, :])


def kernel(x, c, ctx, c_ctx, w_ada, b_ada, w_in, w_s, b_s, ln_v_g, ln_v_b, conv_qk, b_gates, hn_g,
           w_out, ln1_g, ln1_b, w1, b1, w2, b2, ln2_g, ln2_b):
    bsz = x.shape[0]
    l = 0
    cond = jnp.concatenate([c, c_ctx[None, :], jnp.zeros((ADA_ROWS - bsz - 1, D_MODEL), F32)], axis=0)
    mods = _ada(cond, w_ada[l], b_ada[l][None, :])[:, None, :]

    pw = _proj_weights(w_in[l], w_s[l], b_s[l], ln_v_g[l], ln_v_b[l], conv_qk[l], b_gates[l], hn_g[l])
    ya, kk, qt, vt, og, gr = _proj(x, mods, None, pw, PROJ_TILE)
    _, kkc, _, vtc, _, grc = _proj(ctx, mods, bsz, pw, ctx.shape[1])

    gq, gqc = _gates(gr, grc)
    yb = _mlstm(kk, qt, vt, og, gq, kkc, vtc, gqc)

    row = lambda v: v[None, :]
    post_w = (w_out[l].astype(BF16), row(ln1_g[l]), row(ln1_b[l]), w1[l].astype(BF16), row(b1[l]),
              w2[l].astype(BF16), row(b2[l]), row(ln2_g[l]), row(ln2_b[l]))
    return _post(x, ya, yb, mods, post_w, POST_TILE)
```

```python
import math

import jax
import jax.numpy as jnp
from jax import lax
from jax.experimental import pallas as pl
from jax.experimental.pallas import tpu as pltpu

F32 = jnp.float32
BF16 = jnp.bfloat16

D_MODEL = 1024
D_GMLP = 512
G_A = 4
CHUNK = 128
D_MLSTM = 512
H_B = 4
HV = 128
HK = 64
D_QK = 256
D_FF = 4096
DEPTH = 1
ALPHA = (2 * DEPTH) ** 0.25
LN_EPS = 1e-5
LOG2E = math.log2(math.e)

ADA_ROWS = 16
HALO = 16
SUB = 8
N_GATE = 4
Q_C, Q_CM, Q_B, Q_CMAX, Q_BTOT, N_GQ = 0, 2, 4, 6, 8, 10
AUG = 16
HVA = HV + AUG
VMEM_LIMIT_BYTES = 56 * 1024 * 1024
PROJ_TILE = 1024
PROJ_SUB = 256
POST_TILE = 1024
POST_SUB = 256
POST_FF = 1024
OUT_GROUP = 8

NT_DIMS = (((1,), (1,)), ((), ()))


def _ln_stats(x):
    mu = jnp.mean(x, axis=-1, keepdims=True)
    xc = x - mu
    var = jnp.mean(xc * xc, axis=-1, keepdims=True)
    return xc * lax.rsqrt(var + LN_EPS)


def _sigmoid(x):
    return 1.0 / (1.0 + jnp.exp(-x))


def _const_spec(shape):
    nd = len(shape)
    return pl.BlockSpec(shape, lambda *_: (0,) * nd, pipeline_mode=pl.Buffered(1))


def _ada_kernel(c_ref, w_ref, b_ref, o_ref):
    c = c_ref[...]
    s = (c * _sigmoid(c)).astype(BF16)
    o_ref[...] = jnp.dot(s, w_ref[...].astype(BF16), preferred_element_type=F32) + b_ref[...]


def _ada(cond, w, b):
    m, d = cond.shape
    n = w.shape[1]
    bn = 1536
    return pl.pallas_call(
        _ada_kernel,
        grid=(n // bn,),
        in_specs=[pl.BlockSpec((m, d), lambda j: (0, 0)),
                  pl.BlockSpec((d, bn), lambda j: (0, j)),
                  pl.BlockSpec((1, bn), lambda j: (0, j))],
        out_specs=pl.BlockSpec((m, bn), lambda j: (0, j)),
        out_shape=jax.ShapeDtypeStruct((m, n), F32),
        name="ada",
    )(cond, w, b)


def _proj_kernel(x_ref, xp_ref, xn_ref, sh_ref, sc_ref, wuv_ref, wqk_ref, wvt_ref, wo_ref,
                 ws_ref, bs_ref, lnvg_ref, lnvb_ref, cw_ref, qs_ref, bg_ref, hng_ref,
                 ya_ref, kk_ref, qt_ref, vt_ref, og_ref, gr_ref):
    i = pl.program_id(1)
    nt = pl.num_programs(1)
    tm = x_ref.shape[0]
    ts = min(PROJ_SUB, tm)
    nsub = tm // ts
    sc1 = 1.0 + sc_ref[...]
    sh = sh_ref[...]
    cw = cw_ref[...]
    lane = lax.broadcasted_iota(jnp.int32, (ts, CHUNK), 1)

    def mod(xv):
        return _ln_stats(xv) * sc1 + sh

    def project(s, hm, hext):
        r0 = s * ts
        rows = slice(r0, r0 + ts)
        puv = jnp.dot(hm, wuv_ref[...], preferred_element_type=F32)
        pqk = jnp.dot(hext, wqk_ref[...], preferred_element_type=F32)
        vtg = lax.dot_general(wvt_ref[...], hm, NT_DIMS, preferred_element_type=F32)
        vt_ref[:, rows] = vtg[:D_MLSTM, :].astype(BF16)
        og_ref[rows, :] = (_sigmoid(jnp.dot(hm, wo_ref[...], preferred_element_type=F32)) * hng_ref[...]).astype(BF16)

        conv = (pltpu.roll(pqk, 1, 0) * cw[0:1, :] + pqk * cw[1:2, :]
                + pltpu.roll(pqk, ts + 2 * HALO - 1, 0) * cw[2:3, :])
        conv = conv[HALO:HALO + ts, :]
        act = conv * _sigmoid(conv) * qs_ref[...]
        for pair in range(H_B // 2):
            blk = act[:, D_QK + pair * CHUNK:D_QK + (pair + 1) * CHUNK]
            swp = pltpu.roll(blk, HK, 1)
            kk_ref[rows, (2 * pair) * CHUNK:(2 * pair + 1) * CHUNK] = jnp.where(lane < HK, blk, swp).astype(BF16)
            kk_ref[rows, (2 * pair + 1) * CHUNK:(2 * pair + 2) * CHUNK] = jnp.where(lane < HK, swp, blk).astype(BF16)
            for c in range(ts // CHUNK):
                blk = act[c * CHUNK:(c + 1) * CHUNK, pair * CHUNK:(pair + 1) * CHUNK]
                qt_ref[pair * CHUNK:(pair + 1) * CHUNK, r0 + c * CHUNK:r0 + (c + 1) * CHUNK] = blk.T.astype(BF16)

        gt = vtg[D_MLSTM:, :] + bg_ref[...]
        for h in range(H_B):
            for q in range(N_GATE):
                for c in range(ts // CHUNK):
                    jr = r0 // CHUNK + c
                    gr_ref[h, q, jr:jr + 1, :] = gt[h * N_GATE + q:h * N_GATE + q + 1, c * CHUNK:(c + 1) * CHUNK]

        lnv = (_ln_stats(puv[:, D_GMLP:]) * lnvg_ref[...] + lnvb_ref[...]).astype(BF16)
        nch = ts // CHUNK
        for g in range(G_A):
            cols = slice(g * CHUNK, (g + 1) * CHUNK)
            side = jnp.concatenate([lnv[c * CHUNK:(c + 1) * CHUNK, cols] for c in range(nch)], axis=1)
            mixed = jnp.dot(ws_ref[g], side, preferred_element_type=F32)
            for c in range(nch):
                m_c = mixed[:, c * CHUNK:(c + 1) * CHUNK] + bs_ref[g]
                ya_ref[r0 + c * CHUNK:r0 + (c + 1) * CHUNK, cols] = (
                    puv[c * CHUNK:(c + 1) * CHUNK, cols] * m_c).astype(BF16)

    if gr_ref.shape[2] > tm // CHUNK:
        gr_ref[:, :, tm // CHUNK:, :] = jnp.zeros((H_B, N_GATE, gr_ref.shape[2] - tm // CHUNK, CHUNK), F32)
    keep_prev = jnp.where(i > 0, 1.0, 0.0)
    keep_next = jnp.where(i < nt - 1, 1.0, 0.0)
    left = (mod(xp_ref[...]) * keep_prev).astype(BF16)
    cur = mod(x_ref[0:ts, :]).astype(BF16)
    for s in range(nsub):
        if s + 1 < nsub:
            nxt = mod(x_ref[(s + 1) * ts:(s + 2) * ts, :]).astype(BF16)
        else:
            nxt = (mod(xn_ref[...]) * keep_next).astype(BF16)
        project(s, cur, jnp.concatenate([left, cur, nxt[:HALO, :]], axis=0))
        left, cur = cur[ts - HALO:, :], nxt


def _mod_spec(k, row=None):
    return pl.BlockSpec((None, 1, D_MODEL), lambda b, i: (b if row is None else row, 0, k))


def _next_step(bsz, nt):
    def next_step(b, i):
        wrap = (i + 1 == nt).astype(jnp.int32)
        return jnp.minimum(b + wrap, bsz - 1), jnp.where(i + 1 == nt, 0, i + 1)
    return next_step


def _proj(x, mods, mod_row, wts, tm):
    bsz, s, d = x.shape
    nt = s // tm
    hb = tm // HALO
    nhb = s // HALO
    (wuv, wqk, wvt, wo, ws, bs, lnvg, lnvb, cw, qs, bg, hng) = wts
    bg_t = jnp.broadcast_to(bg, (bg.shape[0], min(PROJ_SUB, tm)))
    tok = lambda n: jax.ShapeDtypeStruct((bsz, s, n), BF16)
    tok_spec = lambda n: pl.BlockSpec((None, tm, n), lambda b, i: (b, i, 0))
    tr = lambda n: jax.ShapeDtypeStruct((bsz, n, s), BF16)
    tr_spec = lambda n: pl.BlockSpec((None, n, tm), lambda b, i: (b, 0, i))
    return pl.pallas_call(
        _proj_kernel,
        grid=(bsz, nt),
        in_specs=[
            pl.BlockSpec((None, tm, d), lambda b, i: (b, i, 0)),
            pl.BlockSpec((None, HALO, d), lambda b, i: (b, jnp.maximum(i * hb - 1, 0), 0)),
            pl.BlockSpec((None, HALO, d), lambda b, i: (b, jnp.minimum((i + 1) * hb, nhb - 1), 0)),
            _mod_spec(0, mod_row), _mod_spec(1, mod_row),
            _const_spec(wuv.shape), _const_spec(wqk.shape), _const_spec(wvt.shape), _const_spec(wo.shape),
            _const_spec(ws.shape), _const_spec(bs.shape), _const_spec(lnvg.shape), _const_spec(lnvb.shape),
            _const_spec(cw.shape), _const_spec(qs.shape), _const_spec(bg_t.shape), _const_spec(hng.shape),
        ],
        out_specs=[tok_spec(D_GMLP), tok_spec(2 * D_QK), tr_spec(D_QK), tr_spec(D_MLSTM), tok_spec(D_MLSTM),
                   pl.BlockSpec((None, H_B, N_GATE, max(tm // CHUNK, SUB), CHUNK), lambda b, i: (b, 0, 0, i, 0))],
        out_shape=[tok(D_GMLP), tok(2 * D_QK), tr(D_QK), tr(D_MLSTM), tok(D_MLSTM),
                   jax.ShapeDtypeStruct((bsz, H_B, N_GATE, max(s // CHUNK, SUB), CHUNK), F32)],
        compiler_params=pltpu.CompilerParams(
            dimension_semantics=("parallel", "arbitrary"), vmem_limit_bytes=VMEM_LIMIT_BYTES),
        name="proj",
    )(x, x, x, mods, mods, wuv, wqk, wvt, wo, ws, bs, lnvg, lnvb, cw, qs, bg_t, hng)


def _prefix(x, op, lane, fill):
    d = 1
    while d < CHUNK:
        x = op(x, jnp.where(lane >= d, pltpu.roll(x, d, 1), fill))
        d *= 2
    return x


def _suffix(x, op, lane, fill):
    d = 1
    while d < CHUNK:
        x = op(x, jnp.where(lane < CHUNK - d, pltpu.roll(x, CHUNK - d, 1), fill))
        d *= 2
    return x


def _gates_kernel(gr_ref, grc_ref, gq_ref, gqc_ref):
    for src, dst in ((gr_ref, gq_ref), (grc_ref, gqc_ref)):
        nrow = src.shape[2]
        shape = (H_B * nrow, CHUNK)
        lane = lax.broadcasted_iota(jnp.int32, shape, 1)
        for r, scan in enumerate((_prefix, _suffix)):
            gi = src[:, r, :, :].reshape(shape)
            gf = src[:, 2 + r, :, :].reshape(shape)
            lf = (jnp.minimum(gf, 0.0) - jnp.log(1.0 + jnp.exp(-jnp.abs(gf)))) * LOG2E
            b = scan(lf, jnp.add, lane, 0.0)
            c = gi * LOG2E - b
            planes = ((Q_C, c), (Q_CM, scan(c, jnp.maximum, lane, -jnp.inf)), (Q_B, b),
                      (Q_CMAX, jnp.broadcast_to(jnp.max(c, axis=1, keepdims=True), shape)),
                      (Q_BTOT, jnp.broadcast_to(jnp.min(b, axis=1, keepdims=True), shape)))
            for q, val in planes:
                dst[:, q + r, :, :] = val.reshape(H_B, nrow, CHUNK)


def _gates(gr, grc):
    bsz = gr.shape[0]
    spec = lambda a, n: pl.BlockSpec((None, H_B, n, a.shape[3], CHUNK), lambda b: (b, 0, 0, 0, 0))
    shape = lambda a: jax.ShapeDtypeStruct((bsz, H_B, N_GQ, a.shape[3], CHUNK), F32)
    return pl.pallas_call(
        _gates_kernel,
        grid=(bsz,),
        in_specs=[spec(gr, N_GATE), spec(grc, N_GATE)],
        out_specs=[spec(gr, N_GQ), spec(grc, N_GQ)],
        out_shape=[shape(gr), shape(grc)],
        name="gates",
    )(gr, grc)


def _mlstm_kernel(kk_ref, qt_ref, vt_ref, og_ref, gr_ref, kkc_ref, vtc_ref, grc_ref, yb_ref,
                  m0_s, dc_s, stf_s, stb_s, rhs_s):
    ncx = kkc_ref.shape[0] // CHUNK
    ncl = kk_ref.shape[0] // CHUNK
    aug_rows = jnp.where(lax.broadcasted_iota(jnp.int32, (AUG, CHUNK), 0) == 0, 1.0, 0.0).astype(BF16)
    row = lax.broadcasted_iota(jnp.int32, (SUB, CHUNK), 0)
    lane1 = lax.broadcasted_iota(jnp.int32, (1, CHUNK), 1)
    lane_st = lax.broadcasted_iota(jnp.int32, (HVA, CHUNK), 1)

    def ctx_chunk(jc):
        return jc, (lambda q: grc_ref[q, jc:jc + 1, :]), None

    def lat_chunk(jl):
        return ncx + jl, (lambda q: gr_ref[q, pl.ds(jl, 1), :]), jl

    def delta_state(kk, vt, chunk):
        j, vec, _ = chunk
        vta = jnp.concatenate([vt, aug_rows], axis=0)
        lhs = jnp.concatenate([vta * jnp.exp2(vec(Q_C + r) - vec(Q_CMAX + r)).astype(BF16)
                               for r in range(2)], axis=0)
        d2 = jnp.dot(lhs, kk, preferred_element_type=F32)
        dc_s[j] = jnp.where(lane_st < HK, d2[:HVA, :], d2[HVA:, :])

    for jc in range(ncx):
        sl = slice(jc * CHUNK, (jc + 1) * CHUNK)
        delta_state(kkc_ref[sl, :], vtc_ref[:, sl], ctx_chunk(jc))

    def prep_body(jl, carry):
        t0 = pl.multiple_of(jl * CHUNK, CHUNK)
        delta_state(kk_ref[pl.ds(t0, CHUNK), :], vt_ref[:, pl.ds(t0, CHUNK)], lat_chunk(jl))
        return carry

    lax.fori_loop(0, ncl, prep_body, 0, unroll=8)

    def scan_step(chunks, state, ms):
        stf_s[chunks[0][0]] = state.astype(BF16)
        stb_s[chunks[1][0]] = state.astype(BF16)
        coef, new_ms = [], []
        for r, (_, vec, jl) in enumerate(chunks):
            if jl is not None:
                m0_s[r, pl.ds(jl, 1), :] = ms[r]
            cmax = vec(Q_CMAX + r)
            g = jnp.maximum(ms[r], cmax)
            coef.append((jnp.exp2(ms[r] - g), jnp.exp2(cmax - g)))
            new_ms.append(vec(Q_BTOT + r) + g)
        a_l = jnp.where(lane1 < HK, coef[0][0], coef[1][0])
        b_l = jnp.where(lane1 < HK, coef[0][1], coef[1][1])
        dc = jnp.where(lane_st < HK, dc_s[chunks[0][0]], dc_s[chunks[1][0]])
        return a_l * state + b_l * dc, tuple(new_ms)

    state = jnp.zeros((HVA, CHUNK), F32)
    ms = (jnp.zeros((1, CHUNK), F32), jnp.zeros((1, CHUNK), F32))
    for p in range(ncx):
        state, ms = scan_step((ctx_chunk(p), ctx_chunk(ncx - 1 - p)), state, ms)

    def scan_body(p, carry):
        return scan_step((lat_chunk(p), lat_chunk(ncl - 1 - p)), *carry)

    lax.fori_loop(0, ncl, scan_body, (state, ms), unroll=2)

    si = lax.broadcasted_iota(jnp.int32, (CHUNK, CHUNK), 0)
    ti = lax.broadcasted_iota(jnp.int32, (CHUNK, CHUNK), 1)
    masks = (si <= ti, si >= ti)
    zpad = jnp.zeros((CHUNK - SUB, CHUNK), F32)
    zq = jnp.zeros((HK, CHUNK), BF16)

    def stabiliser(r, jl):
        m0 = m0_s[r, pl.ds(jl, 1), :]
        return m0, jnp.maximum(m0, gr_ref[Q_CM + r, pl.ds(jl, 1), :])

    def weights_stage(jl):
        t0 = pl.multiple_of(jl * CHUNK, CHUNK)
        vec = lat_chunk(jl)[1]
        c2 = jnp.where(row == 0, vec(Q_C), jnp.where(row == 1, vec(Q_C + 1), 0.0))
        ct = jnp.concatenate([c2, zpad], axis=0).T
        qt = qt_ref[:, pl.ds(t0, CHUNK)]
        st = lax.dot_general(kk_ref[pl.ds(t0, CHUNK), :], jnp.concatenate([qt, zq], axis=0),
                             (((1,), (0,)), ((), ())), preferred_element_type=F32)
        qtf = qt.astype(F32)
        for r in range(2):
            m0, g = stabiliser(r, jl)
            w = jnp.where(masks[r], jnp.exp2(ct[:, r:r + 1] - g), 0.0)
            rhs_s[jl, 0:CHUNK, r * CHUNK:(r + 1) * CHUNK] = (st * w).astype(BF16)
            qw = (qtf * jnp.exp2(m0 - g)).astype(BF16)
            rhs_s[jl, CHUNK + r * HK:CHUNK + (r + 1) * HK, r * CHUNK:(r + 1) * CHUNK] = qw
            rhs_s[jl, CHUNK + (1 - r) * HK:CHUNK + (2 - r) * HK, r * CHUNK:(r + 1) * CHUNK] = zq

    def output_stage(jl):
        j = ncx + jl
        t0 = pl.multiple_of(jl * CHUNK, CHUNK)
        vta = jnp.concatenate([vt_ref[:, pl.ds(t0, CHUNK)], aug_rows], axis=0)
        stt = jnp.where(lane_st < HK, stf_s[j], stb_s[j])
        nd = jnp.dot(jnp.concatenate([vta, stt], axis=1), rhs_s[jl], preferred_element_type=F32)
        ht = None
        for r in range(2):
            blk = nd[:, r * CHUNK:(r + 1) * CHUNK]
            fl = jnp.exp2(-(gr_ref[Q_B + r, pl.ds(jl, 1), :] + stabiliser(r, jl)[1]))
            rden = 1.0 / jnp.maximum(jnp.abs(blk[HV:HV + 1, :]), fl)
            part = blk[:HV, :] * rden
            ht = part if ht is None else ht + part
        mu = jnp.mean(ht, axis=0, keepdims=True)
        xc = ht - mu
        var = jnp.mean(xc * xc, axis=0, keepdims=True)
        y = (xc * lax.rsqrt(var + LN_EPS)).T
        yb_ref[pl.ds(t0, CHUNK), :] = y.astype(BF16) * og_ref[pl.ds(t0, CHUNK), :]

    for u in range(OUT_GROUP):
        weights_stage(u)

    def out_body(i, carry):
        for u in range(OUT_GROUP):
            output_stage(i * OUT_GROUP + u)
        for u in range(OUT_GROUP):
            weights_stage((i + 1) * OUT_GROUP + u)
        return carry

    lax.fori_loop(0, ncl // OUT_GROUP - 1, out_body, 0)
    for u in range(OUT_GROUP):
        output_stage(ncl - OUT_GROUP + u)


def _mlstm(kk, qt, vt, og, gr, kkc, vtc, grc):
    bsz, s, _ = kk.shape
    sc = kkc.shape[1]
    ncl = s // CHUNK
    nr = sc // CHUNK + ncl
    head = lambda n: pl.BlockSpec((None, n, CHUNK), lambda b, h: (b, 0, h))
    head_t = lambda rows, n: pl.BlockSpec((None, rows, n), lambda b, h: (b, h, 0))
    gates = lambda a: pl.BlockSpec((None, None, N_GQ, a.shape[3], CHUNK), lambda b, h: (b, h, 0, 0, 0))
    return pl.pallas_call(
        _mlstm_kernel,
        grid=(bsz, H_B),
        in_specs=[head(s), head_t(HK, s), head_t(HV, s), head(s), gates(gr),
                  head(sc), head_t(HV, sc), gates(grc)],
        out_specs=head(s),
        out_shape=jax.ShapeDtypeStruct((bsz, s, D_MLSTM), BF16),
        scratch_shapes=[pltpu.VMEM((2, ncl, CHUNK), F32),
                        pltpu.VMEM((nr, HVA, CHUNK), F32),
                        pltpu.VMEM((nr, HVA, CHUNK), BF16), pltpu.VMEM((nr, HVA, CHUNK), BF16),
                        pltpu.VMEM((ncl, 2 * CHUNK, 2 * CHUNK), BF16)],
        compiler_params=pltpu.CompilerParams(
            dimension_semantics=("parallel", "arbitrary"), vmem_limit_bytes=VMEM_LIMIT_BYTES),
        name="mlstm",
    )(kk, qt, vt, og, gr, kkc, vtc, grc)


def _post_kernel(x_ref, ya_ref, yb_ref, g1_ref, sh2_ref, sc2_ref, g2_ref,
                 xn_ref, yan_ref, ybn_ref, g1n_ref, sh2n_ref, sc2n_ref,
                 wo_ref, l1g_ref, l1b_ref, w1_ref, b1_ref, w2_ref, b2_ref, l2g_ref, l2b_ref,
                 out_ref, x1c_s, h2c_s):
    nf = POST_FF
    nparts = D_FF // nf
    bounds = list(range(0, x_ref.shape[0] + 1, POST_SUB))
    nsub = len(bounds) - 1

    def head_of(xv, ya, yb, g1, sh2, sc2):
        y = jnp.dot(jnp.concatenate([ya, yb], axis=1), wo_ref[...], preferred_element_type=F32)
        x1 = _ln_stats(ALPHA * xv + g1 * y) * l1g_ref[...] + l1b_ref[...]
        h2 = (_ln_stats(x1) * (1.0 + sc2) + sh2).astype(BF16)
        return x1, h2

    def head(r):
        rows = slice(bounds[r], bounds[r + 1])
        return head_of(x_ref[rows, :], ya_ref[rows, :], yb_ref[rows, :], g1_ref[...], sh2_ref[...], sc2_ref[...])

    def head_next():
        x1, h2 = head_of(xn_ref[...], yan_ref[...], ybn_ref[...], g1n_ref[...], sh2n_ref[...], sc2n_ref[...])
        x1c_s[...] = x1
        h2c_s[...] = h2

    @pl.when((pl.program_id(0) == 0) & (pl.program_id(1) == 0))
    def _():
        x1, h2 = head(0)
        x1c_s[...] = x1
        h2c_s[...] = h2

    def up(h2, f):
        cols = slice(f * nf, (f + 1) * nf)
        return jnp.dot(h2, w1_ref[:, cols], preferred_element_type=F32) + b1_ref[:, cols]

    def down(t, f):
        t = jnp.maximum(t, 0.0)
        return jnp.dot((t * t).astype(BF16), w2_ref[f * nf:(f + 1) * nf, :], preferred_element_type=F32)

    def tail(r, x1, z):
        rows = slice(bounds[r], bounds[r + 1])
        out_ref[rows, :] = _ln_stats(ALPHA * x1 + g2_ref[...] * (z + b2_ref[...])) * l2g_ref[...] + l2b_ref[...]

    cur = (x1c_s[...], h2c_s[...])
    pending = None
    for r in range(nsub):
        x1, h2 = cur
        t = up(h2, 0)
        z = None
        for f in range(nparts):
            t_next = up(h2, f + 1) if f + 1 < nparts else None
            if f == min(1, nparts - 1) and pending is not None:
                tail(*pending)
            if f == min(2, nparts - 1):
                if r + 1 < nsub:
                    cur = head(r + 1)
                else:
                    head_next()
            d = down(t, f)
            z = d if z is None else z + d
            t = t_next
        pending = (r, x1, z)
    tail(*pending)


def _post(x, ya, yb, mods, wts, tm):
    bsz, s, d = x.shape
    (wo, l1g, l1b, w1, b1, w2, b2, l2g, l2b) = wts
    nt = s // tm
    tok_spec = lambda n: pl.BlockSpec((None, tm, n), lambda b, i: (b, i, 0))

    next_step = _next_step(bsz, nt)

    def next_tok_spec(n):
        def index(b, i):
            b2, i2 = next_step(b, i)
            return b2, i2 * (tm // POST_SUB), 0
        return pl.BlockSpec((None, POST_SUB, n), index)

    def next_mod_spec(k):
        return pl.BlockSpec((None, 1, D_MODEL), lambda b, i: (next_step(b, i)[0], 0, k))

    return pl.pallas_call(
        _post_kernel,
        grid=(bsz, nt),
        in_specs=[tok_spec(d), tok_spec(D_GMLP), tok_spec(D_MLSTM),
                  _mod_spec(2), _mod_spec(3), _mod_spec(4), _mod_spec(5),
                  next_tok_spec(d), next_tok_spec(D_GMLP), next_tok_spec(D_MLSTM),
                  next_mod_spec(2), next_mod_spec(3), next_mod_spec(4)]
                 + [_const_spec(w.shape) for w in wts],
        out_specs=tok_spec(d),
        out_shape=jax.ShapeDtypeStruct((bsz, s, d), F32),
        scratch_shapes=[pltpu.VMEM((POST_SUB, d), F32), pltpu.VMEM((POST_SUB, d), BF16)],
        compiler_params=pltpu.CompilerParams(
            dimension_semantics=("arbitrary", "arbitrary"), vmem_limit_bytes=VMEM_LIMIT_BYTES),
        name="post",
    )(x, ya, yb, mods, mods, mods, mods, x, ya, yb, mods, mods, mods, wo, l1g, l1b, w1, b1, w2, b2, l2g, l2b)


def _proj_weights(w_in, w_s, b_s, ln_v_g, ln_v_b, conv_qk, b_gates, hn_g):
    q0 = 2 * D_GMLP
    v0 = q0 + 2 * D_QK
    o0 = v0 + D_MLSTM
    g0 = o0 + D_MLSTM
    def gate_tiles(a):
        return a.reshape(2, 2, H_B, -1).transpose(2, 1, 0, 3).reshape(H_B * N_GATE, -1)

    wuv = w_in[:, :q0].astype(BF16)
    wqk = w_in[:, q0:v0].astype(BF16)
    wvt = jnp.concatenate([w_in[:, v0:o0].T, gate_tiles(w_in[:, g0:].T)], axis=0).astype(BF16)
    wo = w_in[:, o0:g0].astype(BF16)
    ws = w_s.astype(BF16)
    bs = jnp.broadcast_to(b_s[:, :, None], (G_A, CHUNK, CHUNK))
    qs = jnp.concatenate([jnp.full((D_QK,), HK ** -0.5, F32), jnp.ones((D_QK,), F32)])[None, :]
    return (wuv, wqk, wvt, wo, ws, bs, ln_v_g[None, :], ln_v_b[None, :], conv_qk, qs, gate_tiles(b_gates[:, None]),
            hn_g[None, :])


def kernel(x, c, ctx, c_ctx, w_ada, b_ada, w_in, w_s, b_s, ln_v_g, ln_v_b, conv_qk, b_gates, hn_g,
           w_out, ln1_g, ln1_b, w1, b1, w2, b2, ln2_g, ln2_b):
    bsz = x.shape[0]
    l = 0
    cond = jnp.concatenate([c, c_ctx[None, :], jnp.zeros((ADA_ROWS - bsz - 1, D_MODEL), F32)], axis=0)
    mods = _ada(cond, w_ada[l], b_ada[l][None, :])[:, None, :]

    pw = _proj_weights(w_in[l], w_s[l], b_s[l], ln_v_g[l], ln_v_b[l], conv_qk[l], b_gates[l], hn_g[l])
    ya, kk, qt, vt, og, gr = _proj(x, mods, None, pw, PROJ_TILE)
    _, kkc, _, vtc, _, grc = _proj(ctx, mods, bsz, pw, ctx.shape[1])

    gq, gqc = _gates(gr, grc)
    yb = _mlstm(kk, qt, vt, og, gq, kkc, vtc, gqc)

    row = lambda v: v[None, :]
    post_w = (w_out[l].astype(BF16), row(ln1_g[l]), row(ln1_b[l]), w1[l].astype(BF16), row(b1[l]),
              w2[l].astype(BF16), row(b2[l]), row(ln2_g[l]), row(ln2_b[l]))
    return _post(x, ya, yb, mods, post_w, POST_TILE)
```

```python
import math

import jax
import jax.numpy as jnp
from jax import lax
from jax.experimental import pallas as pl
from jax.experimental.pallas import tpu as pltpu

F32 = jnp.float32
BF16 = jnp.bfloat16

D_MODEL = 1024
D_GMLP = 512
G_A = 4
CHUNK = 128
D_MLSTM = 512
H_B = 4
HV = 128
HK = 64
D_QK = 256
D_FF = 4096
DEPTH = 1
ALPHA = (2 * DEPTH) ** 0.25
LN_EPS = 1e-5
LOG2E = math.log2(math.e)

ADA_ROWS = 16
HALO = 16
SUB = 8
N_GATE = 4
Q_C, Q_CM, Q_B, Q_CMAX, Q_BTOT, N_GQ = 0, 2, 4, 6, 8, 10
AUG = 16
HVA = HV + AUG
VMEM_LIMIT_BYTES = 56 * 1024 * 1024
PROJ_TILE = 2048
PROJ_SUB = 256
POST_TILE = 1024
POST_SUB = 256
POST_FF = 1024
OUT_GROUP = 16

NT_DIMS = (((1,), (1,)), ((), ()))


def _ln_stats(x):
    mu = jnp.mean(x, axis=-1, keepdims=True)
    xc = x - mu
    var = jnp.mean(xc * xc, axis=-1, keepdims=True)
    return xc * lax.rsqrt(var + LN_EPS)


def _sigmoid(x):
    return 1.0 / (1.0 + jnp.exp(-x))


def _const_spec(shape):
    nd = len(shape)
    return pl.BlockSpec(shape, lambda *_: (0,) * nd, pipeline_mode=pl.Buffered(1))


def _ada_kernel(c_ref, w_ref, b_ref, o_ref):
    c = c_ref[...]
    s = (c * _sigmoid(c)).astype(BF16)
    o_ref[...] = jnp.dot(s, w_ref[...].astype(BF16), preferred_element_type=F32) + b_ref[...]


def _ada(cond, w, b):
    m, d = cond.shape
    n = w.shape[1]
    bn = 1536
    return pl.pallas_call(
        _ada_kernel,
        grid=(n // bn,),
        in_specs=[pl.BlockSpec((m, d), lambda j: (0, 0)),
                  pl.BlockSpec((d, bn), lambda j: (0, j)),
                  pl.BlockSpec((1, bn), lambda j: (0, j))],
        out_specs=pl.BlockSpec((m, bn), lambda j: (0, j)),
        out_shape=jax.ShapeDtypeStruct((m, n), F32),
        name="ada",
    )(cond, w, b)


def _proj_kernel(x_ref, xp_ref, xn_ref, sh_ref, sc_ref, wuv_ref, wqk_ref, wvt_ref, wo_ref,
                 ws_ref, bs_ref, lnvg_ref, lnvb_ref, cw_ref, qs_ref, bg_ref, hng_ref,
                 ya_ref, kk_ref, qt_ref, vt_ref, og_ref, gr_ref):
    i = pl.program_id(1)
    nt = pl.num_programs(1)
    tm = x_ref.shape[0]
    ts = min(PROJ_SUB, tm)
    nsub = tm // ts
    sc1 = 1.0 + sc_ref[...]
    sh = sh_ref[...]
    cw = cw_ref[...]
    lane = lax.broadcasted_iota(jnp.int32, (ts, CHUNK), 1)

    def mod(xv):
        return _ln_stats(xv) * sc1 + sh

    def project(s, hm, hext):
        r0 = s * ts
        rows = slice(r0, r0 + ts)
        puv = jnp.dot(hm, wuv_ref[...], preferred_element_type=F32)
        pqk = jnp.dot(hext, wqk_ref[...], preferred_element_type=F32)
        vtg = lax.dot_general(wvt_ref[...], hm, NT_DIMS, preferred_element_type=F32)
        vt_ref[:, rows] = vtg[:D_MLSTM, :].astype(BF16)
        og_ref[rows, :] = (_sigmoid(jnp.dot(hm, wo_ref[...], preferred_element_type=F32)) * hng_ref[...]).astype(BF16)

        conv = (pltpu.roll(pqk, 1, 0) * cw[0:1, :] + pqk * cw[1:2, :]
                + pltpu.roll(pqk, ts + 2 * HALO - 1, 0) * cw[2:3, :])
        conv = conv[HALO:HALO + ts, :]
        act = conv * _sigmoid(conv) * qs_ref[...]
        for pair in range(H_B // 2):
            blk = act[:, D_QK + pair * CHUNK:D_QK + (pair + 1) * CHUNK]
            swp = pltpu.roll(blk, HK, 1)
            kk_ref[rows, (2 * pair) * CHUNK:(2 * pair + 1) * CHUNK] = jnp.where(lane < HK, blk, swp).astype(BF16)
            kk_ref[rows, (2 * pair + 1) * CHUNK:(2 * pair + 2) * CHUNK] = jnp.where(lane < HK, swp, blk).astype(BF16)
            for c in range(ts // CHUNK):
                blk = act[c * CHUNK:(c + 1) * CHUNK, pair * CHUNK:(pair + 1) * CHUNK]
                qt_ref[pair * CHUNK:(pair + 1) * CHUNK, r0 + c * CHUNK:r0 + (c + 1) * CHUNK] = blk.T.astype(BF16)

        gt = vtg[D_MLSTM:, :] + bg_ref[...]
        for h in range(H_B):
            for q in range(N_GATE):
                for c in range(ts // CHUNK):
                    jr = r0 // CHUNK + c
                    gr_ref[h, q, jr:jr + 1, :] = gt[h * N_GATE + q:h * N_GATE + q + 1, c * CHUNK:(c + 1) * CHUNK]

        lnv = (_ln_stats(puv[:, D_GMLP:]) * lnvg_ref[...] + lnvb_ref[...]).astype(BF16)
        nch = ts // CHUNK
        for g in range(G_A):
            cols = slice(g * CHUNK, (g + 1) * CHUNK)
            side = jnp.concatenate([lnv[c * CHUNK:(c + 1) * CHUNK, cols] for c in range(nch)], axis=1)
            mixed = jnp.dot(ws_ref[g], side, preferred_element_type=F32)
            for c in range(nch):
                m_c = mixed[:, c * CHUNK:(c + 1) * CHUNK] + bs_ref[g]
                ya_ref[r0 + c * CHUNK:r0 + (c + 1) * CHUNK, cols] = (
                    puv[c * CHUNK:(c + 1) * CHUNK, cols] * m_c).astype(BF16)

    if gr_ref.shape[2] > tm // CHUNK:
        gr_ref[:, :, tm // CHUNK:, :] = jnp.zeros((H_B, N_GATE, gr_ref.shape[2] - tm // CHUNK, CHUNK), F32)
    keep_prev = jnp.where(i > 0, 1.0, 0.0)
    keep_next = jnp.where(i < nt - 1, 1.0, 0.0)
    left = (mod(xp_ref[...]) * keep_prev).astype(BF16)
    cur = mod(x_ref[0:ts, :]).astype(BF16)
    for s in range(nsub):
        if s + 1 < nsub:
            nxt = mod(x_ref[(s + 1) * ts:(s + 2) * ts, :]).astype(BF16)
        else:
            nxt = (mod(xn_ref[...]) * keep_next).astype(BF16)
        project(s, cur, jnp.concatenate([left, cur, nxt[:HALO, :]], axis=0))
        left, cur = cur[ts - HALO:, :], nxt


def _mod_spec(k, row=None):
    return pl.BlockSpec((None, 1, D_MODEL), lambda b, i: (b if row is None else row, 0, k))


def _proj(x, mods, mod_row, wts, tm):
    bsz, s, d = x.shape
    nt = s // tm
    hb = tm // HALO
    nhb = s // HALO
    (wuv, wqk, wvt, wo, ws, bs, lnvg, lnvb, cw, qs, bg, hng) = wts
    bg_t = jnp.broadcast_to(bg, (bg.shape[0], min(PROJ_SUB, tm)))
    tok = lambda n: jax.ShapeDtypeStruct((bsz, s, n), BF16)
    tok_spec = lambda n: pl.BlockSpec((None, tm, n), lambda b, i: (b, i, 0))
    tr = lambda n: jax.ShapeDtypeStruct((bsz, n, s), BF16)
    tr_spec = lambda n: pl.BlockSpec((None, n, tm), lambda b, i: (b, 0, i))
    return pl.pallas_call(
        _proj_kernel,
        grid=(bsz, nt),
        in_specs=[
            pl.BlockSpec((None, tm, d), lambda b, i: (b, i, 0)),
            pl.BlockSpec((None, HALO, d), lambda b, i: (b, jnp.maximum(i * hb - 1, 0), 0)),
            pl.BlockSpec((None, HALO, d), lambda b, i: (b, jnp.minimum((i + 1) * hb, nhb - 1), 0)),
            _mod_spec(0, mod_row), _mod_spec(1, mod_row),
            _const_spec(wuv.shape), _const_spec(wqk.shape), _const_spec(wvt.shape), _const_spec(wo.shape),
            _const_spec(ws.shape), _const_spec(bs.shape), _const_spec(lnvg.shape), _const_spec(lnvb.shape),
            _const_spec(cw.shape), _const_spec(qs.shape), _const_spec(bg_t.shape), _const_spec(hng.shape),
        ],
        out_specs=[tok_spec(D_GMLP), tok_spec(2 * D_QK), tr_spec(D_QK), tr_spec(D_MLSTM), tok_spec(D_MLSTM),
                   pl.BlockSpec((None, H_B, N_GATE, max(tm // CHUNK, SUB), CHUNK), lambda b, i: (b, 0, 0, i, 0))],
        out_shape=[tok(D_GMLP), tok(2 * D_QK), tr(D_QK), tr(D_MLSTM), tok(D_MLSTM),
                   jax.ShapeDtypeStruct((bsz, H_B, N_GATE, max(s // CHUNK, SUB), CHUNK), F32)],
        compiler_params=pltpu.CompilerParams(
            dimension_semantics=("parallel", "arbitrary"), vmem_limit_bytes=VMEM_LIMIT_BYTES),
        name="proj",
    )(x, x, x, mods, mods, wuv, wqk, wvt, wo, ws, bs, lnvg, lnvb, cw, qs, bg_t, hng)


def _prefix(x, op, lane, fill):
    d = 1
    while d < CHUNK:
        x = op(x, jnp.where(lane >= d, pltpu.roll(x, d, 1), fill))
        d *= 2
    return x


def _suffix(x, op, lane, fill):
    d = 1
    while d < CHUNK:
        x = op(x, jnp.where(lane < CHUNK - d, pltpu.roll(x, CHUNK - d, 1), fill))
        d *= 2
    return x


def _gates_kernel(gr_ref, grc_ref, gq_ref, gqc_ref):
    for src, dst in ((gr_ref, gq_ref), (grc_ref, gqc_ref)):
        nrow = src.shape[2]
        shape = (H_B * nrow, CHUNK)
        lane = lax.broadcasted_iota(jnp.int32, shape, 1)
        for r, scan in enumerate((_prefix, _suffix)):
            gi = src[:, r, :, :].reshape(shape)
            gf = src[:, 2 + r, :, :].reshape(shape)
            lf = (jnp.minimum(gf, 0.0) - jnp.log(1.0 + jnp.exp(-jnp.abs(gf)))) * LOG2E
            b = scan(lf, jnp.add, lane, 0.0)
            c = gi * LOG2E - b
            planes = ((Q_C, c), (Q_CM, scan(c, jnp.maximum, lane, -jnp.inf)), (Q_B, b),
                      (Q_CMAX, jnp.broadcast_to(jnp.max(c, axis=1, keepdims=True), shape)),
                      (Q_BTOT, jnp.broadcast_to(jnp.min(b, axis=1, keepdims=True), shape)))
            for q, val in planes:
                dst[:, q + r, :, :] = val.reshape(H_B, nrow, CHUNK)


def _gates(gr, grc):
    bsz = gr.shape[0]
    spec = lambda a, n: pl.BlockSpec((None, H_B, n, a.shape[3], CHUNK), lambda b: (b, 0, 0, 0, 0))
    shape = lambda a: jax.ShapeDtypeStruct((bsz, H_B, N_GQ, a.shape[3], CHUNK), F32)
    return pl.pallas_call(
        _gates_kernel,
        grid=(bsz,),
        in_specs=[spec(gr, N_GATE), spec(grc, N_GATE)],
        out_specs=[spec(gr, N_GQ), spec(grc, N_GQ)],
        out_shape=[shape(gr), shape(grc)],
        name="gates",
    )(gr, grc)


def _mlstm_kernel(kk_ref, qt_ref, vt_ref, og_ref, gr_ref, kkc_ref, vtc_ref, grc_ref, yb_ref,
                  m0_s, dc_s, stf_s, stb_s, rhs_s):
    ncx = kkc_ref.shape[0] // CHUNK
    ncl = kk_ref.shape[0] // CHUNK
    aug_rows = jnp.where(lax.broadcasted_iota(jnp.int32, (AUG, CHUNK), 0) == 0, 1.0, 0.0).astype(BF16)
    row = lax.broadcasted_iota(jnp.int32, (SUB, CHUNK), 0)
    lane1 = lax.broadcasted_iota(jnp.int32, (1, CHUNK), 1)
    lane_st = lax.broadcasted_iota(jnp.int32, (HVA, CHUNK), 1)

    def ctx_chunk(jc):
        return jc, (lambda q: grc_ref[q, jc:jc + 1, :]), None

    def lat_chunk(jl):
        return ncx + jl, (lambda q: gr_ref[q, pl.ds(jl, 1), :]), jl

    def delta_state(kk, vt, chunk):
        j, vec, _ = chunk
        vta = jnp.concatenate([vt, aug_rows], axis=0)
        lhs = jnp.concatenate([vta * jnp.exp2(vec(Q_C + r) - vec(Q_CMAX + r)).astype(BF16)
                               for r in range(2)], axis=0)
        d2 = jnp.dot(lhs, kk, preferred_element_type=F32)
        dc_s[j] = jnp.where(lane_st < HK, d2[:HVA, :], d2[HVA:, :])

    for jc in range(ncx):
        sl = slice(jc * CHUNK, (jc + 1) * CHUNK)
        delta_state(kkc_ref[sl, :], vtc_ref[:, sl], ctx_chunk(jc))

    def prep_body(jl, carry):
        t0 = pl.multiple_of(jl * CHUNK, CHUNK)
        delta_state(kk_ref[pl.ds(t0, CHUNK), :], vt_ref[:, pl.ds(t0, CHUNK)], lat_chunk(jl))
        return carry

    lax.fori_loop(0, ncl, prep_body, 0, unroll=8)

    def scan_step(chunks, state, ms):
        stf_s[chunks[0][0]] = state.astype(BF16)
        stb_s[chunks[1][0]] = state.astype(BF16)
        coef, new_ms = [], []
        for r, (_, vec, jl) in enumerate(chunks):
            if jl is not None:
                m0_s[r, pl.ds(jl, 1), :] = ms[r]
            cmax = vec(Q_CMAX + r)
            g = jnp.maximum(ms[r], cmax)
            coef.append((jnp.exp2(ms[r] - g), jnp.exp2(cmax - g)))
            new_ms.append(vec(Q_BTOT + r) + g)
        a_l = jnp.where(lane1 < HK, coef[0][0], coef[1][0])
        b_l = jnp.where(lane1 < HK, coef[0][1], coef[1][1])
        dc = jnp.where(lane_st < HK, dc_s[chunks[0][0]], dc_s[chunks[1][0]])
        return a_l * state + b_l * dc, tuple(new_ms)

    state = jnp.zeros((HVA, CHUNK), F32)
    ms = (jnp.zeros((1, CHUNK), F32), jnp.zeros((1, CHUNK), F32))
    for p in range(ncx):
        state, ms = scan_step((ctx_chunk(p), ctx_chunk(ncx - 1 - p)), state, ms)

    def scan_body(p, carry):
        return scan_step((lat_chunk(p), lat_chunk(ncl - 1 - p)), *carry)

    lax.fori_loop(0, ncl, scan_body, (state, ms), unroll=2)

    si = lax.broadcasted_iota(jnp.int32, (CHUNK, CHUNK), 0)
    ti = lax.broadcasted_iota(jnp.int32, (CHUNK, CHUNK), 1)
    masks = (si <= ti, si >= ti)
    zpad = jnp.zeros((CHUNK - SUB, CHUNK), F32)
    zq = jnp.zeros((HK, CHUNK), BF16)

    def stabiliser(r, jl):
        m0 = m0_s[r, pl.ds(jl, 1), :]
        return m0, jnp.maximum(m0, gr_ref[Q_CM + r, pl.ds(jl, 1), :])

    def weights_stage(jl):
        t0 = pl.multiple_of(jl * CHUNK, CHUNK)
        vec = lat_chunk(jl)[1]
        c2 = jnp.where(row == 0, vec(Q_C), jnp.where(row == 1, vec(Q_C + 1), 0.0))
        ct = jnp.concatenate([c2, zpad], axis=0).T
        qt = qt_ref[:, pl.ds(t0, CHUNK)]
        st = lax.dot_general(kk_ref[pl.ds(t0, CHUNK), :], jnp.concatenate([qt, zq], axis=0),
                             (((1,), (0,)), ((), ())), preferred_element_type=F32)
        qtf = qt.astype(F32)
        for r in range(2):
            m0, g = stabiliser(r, jl)
            w = jnp.where(masks[r], jnp.exp2(ct[:, r:r + 1] - g), 0.0)
            rhs_s[jl, 0:CHUNK, r * CHUNK:(r + 1) * CHUNK] = (st * w).astype(BF16)
            qw = (qtf * jnp.exp2(m0 - g)).astype(BF16)
            rhs_s[jl, CHUNK + r * HK:CHUNK + (r + 1) * HK, r * CHUNK:(r + 1) * CHUNK] = qw
            rhs_s[jl, CHUNK + (1 - r) * HK:CHUNK + (2 - r) * HK, r * CHUNK:(r + 1) * CHUNK] = zq

    def output_stage(jl):
        j = ncx + jl
        t0 = pl.multiple_of(jl * CHUNK, CHUNK)
        vta = jnp.concatenate([vt_ref[:, pl.ds(t0, CHUNK)], aug_rows], axis=0)
        stt = jnp.where(lane_st < HK, stf_s[j], stb_s[j])
        nd = jnp.dot(jnp.concatenate([vta, stt], axis=1), rhs_s[jl], preferred_element_type=F32)
        ht = None
        for r in range(2):
            blk = nd[:, r * CHUNK:(r + 1) * CHUNK]
            fl = jnp.exp2(-(gr_ref[Q_B + r, pl.ds(jl, 1), :] + stabiliser(r, jl)[1]))
            rden = 1.0 / jnp.maximum(jnp.abs(blk[HV:HV + 1, :]), fl)
            part = blk[:HV, :] * rden
            ht = part if ht is None else ht + part
        mu = jnp.mean(ht, axis=0, keepdims=True)
        xc = ht - mu
        var = jnp.mean(xc * xc, axis=0, keepdims=True)
        y = (xc * lax.rsqrt(var + LN_EPS)).T
        yb_ref[pl.ds(t0, CHUNK), :] = y.astype(BF16) * og_ref[pl.ds(t0, CHUNK), :]

    for u in range(OUT_GROUP):
        weights_stage(u)

    def out_body(i, carry):
        for u in range(OUT_GROUP):
            output_stage(i * OUT_GROUP + u)
        for u in range(OUT_GROUP):
            weights_stage((i + 1) * OUT_GROUP + u)
        return carry

    lax.fori_loop(0, ncl // OUT_GROUP - 1, out_body, 0)
    for u in range(OUT_GROUP):
        output_stage(ncl - OUT_GROUP + u)


def _mlstm(kk, qt, vt, og, gr, kkc, vtc, grc):
    bsz, s, _ = kk.shape
    sc = kkc.shape[1]
    ncl = s // CHUNK
    nr = sc // CHUNK + ncl
    head = lambda n: pl.BlockSpec((None, n, CHUNK), lambda b, h: (b, 0, h))
    head_t = lambda rows, n: pl.BlockSpec((None, rows, n), lambda b, h: (b, h, 0))
    gates = lambda a: pl.BlockSpec((None, None, N_GQ, a.shape[3], CHUNK), lambda b, h: (b, h, 0, 0, 0))
    return pl.pallas_call(
        _mlstm_kernel,
        grid=(bsz, H_B),
        in_specs=[head(s), head_t(HK, s), head_t(HV, s), head(s), gates(gr),
                  head(sc), head_t(HV, sc), gates(grc)],
        out_specs=head(s),
        out_shape=jax.ShapeDtypeStruct((bsz, s, D_MLSTM), BF16),
        scratch_shapes=[pltpu.VMEM((2, ncl, CHUNK), F32),
                        pltpu.VMEM((nr, HVA, CHUNK), F32),
                        pltpu.VMEM((nr, HVA, CHUNK), BF16), pltpu.VMEM((nr, HVA, CHUNK), BF16),
                        pltpu.VMEM((ncl, 2 * CHUNK, 2 * CHUNK), BF16)],
        compiler_params=pltpu.CompilerParams(
            dimension_semantics=("parallel", "arbitrary"), vmem_limit_bytes=VMEM_LIMIT_BYTES),
        name="mlstm",
    )(kk, qt, vt, og, gr, kkc, vtc, grc)


def _post_kernel(x_ref, ya_ref, yb_ref, g1_ref, sh2_ref, sc2_ref, g2_ref, wo_ref, l1g_ref, l1b_ref,
                 w1_ref, b1_ref, w2_ref, b2_ref, l2g_ref, l2b_ref, out_ref):
    nf = POST_FF
    nparts = D_FF // nf
    bounds = list(range(0, x_ref.shape[0] + 1, POST_SUB))
    nsub = len(bounds) - 1

    def head(r):
        rows = slice(bounds[r], bounds[r + 1])
        yab = jnp.concatenate([ya_ref[rows, :], yb_ref[rows, :]], axis=1)
        y = jnp.dot(yab, wo_ref[...], preferred_element_type=F32)
        x1 = _ln_stats(ALPHA * x_ref[rows, :] + g1_ref[...] * y) * l1g_ref[...] + l1b_ref[...]
        h2 = (_ln_stats(x1) * (1.0 + sc2_ref[...]) + sh2_ref[...]).astype(BF16)
        return x1, h2

    def up(h2, f):
        cols = slice(f * nf, (f + 1) * nf)
        return jnp.dot(h2, w1_ref[:, cols], preferred_element_type=F32) + b1_ref[:, cols]

    def down(t, f):
        t = jnp.maximum(t, 0.0)
        return jnp.dot((t * t).astype(BF16), w2_ref[f * nf:(f + 1) * nf, :], preferred_element_type=F32)

    def tail(r, x1, z):
        rows = slice(bounds[r], bounds[r + 1])
        out_ref[rows, :] = _ln_stats(ALPHA * x1 + g2_ref[...] * (z + b2_ref[...])) * l2g_ref[...] + l2b_ref[...]

    cur = head(0)
    pending = None
    for r in range(nsub):
        x1, h2 = cur
        t = up(h2, 0)
        z = None
        for f in range(nparts):
            t_next = up(h2, f + 1) if f + 1 < nparts else None
            if f == min(1, nparts - 1) and pending is not None:
                tail(*pending)
            if f == min(2, nparts - 1) and r + 1 < nsub:
                cur = head(r + 1)
            d = down(t, f)
            z = d if z is None else z + d
            t = t_next
        pending = (r, x1, z)
    tail(*pending)


def _post(x, ya, yb, mods, wts, tm):
    bsz, s, d = x.shape
    (wo, l1g, l1b, w1, b1, w2, b2, l2g, l2b) = wts
    tok_spec = lambda n: pl.BlockSpec((None, tm, n), lambda b, i: (b, i, 0))
    return pl.pallas_call(
        _post_kernel,
        grid=(bsz, s // tm),
        in_specs=[tok_spec(d), tok_spec(D_GMLP), tok_spec(D_MLSTM),
                  _mod_spec(2), _mod_spec(3), _mod_spec(4), _mod_spec(5)]
                 + [_const_spec(w.shape) for w in wts],
        out_specs=tok_spec(d),
        out_shape=jax.ShapeDtypeStruct((bsz, s, d), F32),
        compiler_params=pltpu.CompilerParams(
            dimension_semantics=("parallel", "arbitrary"), vmem_limit_bytes=VMEM_LIMIT_BYTES),
        name="post",
    )(x, ya, yb, mods, mods, mods, mods, wo, l1g, l1b, w1, b1, w2, b2, l2g, l2b)


def _proj_weights(w_in, w_s, b_s, ln_v_g, ln_v_b, conv_qk, b_gates, hn_g):
    q0 = 2 * D_GMLP
    v0 = q0 + 2 * D_QK
    o0 = v0 + D_MLSTM
    g0 = o0 + D_MLSTM
    def gate_tiles(a):
        return a.reshape(2, 2, H_B, -1).transpose(2, 1, 0, 3).reshape(H_B * N_GATE, -1)

    wuv = w_in[:, :q0].astype(BF16)
    wqk = w_in[:, q0:v0].astype(BF16)
    wvt = jnp.concatenate([w_in[:, v0:o0].T, gate_tiles(w_in[:, g0:].T)], axis=0).astype(BF16)
    wo = w_in[:, o0:g0].astype(BF16)
    ws = w_s.astype(BF16)
    bs = jnp.broadcast_to(b_s[:, :, None], (G_A, CHUNK, CHUNK))
    qs = jnp.concatenate([jnp.full((D_QK,), HK ** -0.5, F32), jnp.ones((D_QK,), F32)])[None, :]
    return (wuv, wqk, wvt, wo, ws, bs, ln_v_g[None, :], ln_v_b[None, :], conv_qk, qs, gate_tiles(b_gates[:, None]),
            hn_g[None, :])


def kernel(x, c, ctx, c_ctx, w_ada, b_ada, w_in, w_s, b_s, ln_v_g, ln_v_b, conv_qk, b_gates, hn_g,
           w_out, ln1_g, ln1_b, w1, b1, w2, b2, ln2_g, ln2_b):
    bsz = x.shape[0]
    l = 0
    cond = jnp.concatenate([c, c_ctx[None, :], jnp.zeros((ADA_ROWS - bsz - 1, D_MODEL), F32)], axis=0)
    mods = _ada(cond, w_ada[l], b_ada[l][None, :])[:, None, :]

    pw = _proj_weights(w_in[l], w_s[l], b_s[l], ln_v_g[l], ln_v_b[l], conv_qk[l], b_gates[l], hn_g[l])
    ya, kk, qt, vt, og, gr = _proj(x, mods, None, pw, PROJ_TILE)
    _, kkc, _, vtc, _, grc = _proj(ctx, mods, bsz, pw, ctx.shape[1])

    gq, gqc = _gates(gr, grc)
    yb = _mlstm(kk, qt, vt, og, gq, kkc, vtc, gqc)

    row = lambda v: v[None, :]
    post_w = (w_out[l].astype(BF16), row(ln1_g[l]), row(ln1_b[l]), w1[l].astype(BF16), row(b1[l]),
              w2[l].astype(BF16), row(b2[l]), row(ln2_g[l]), row(ln2_b[l]))
    return _post(x, ya, yb, mods, post_w, POST_TILE)
```

```python
import math

import jax
import jax.numpy as jnp
from jax import lax
from jax.experimental import pallas as pl
from jax.experimental.pallas import tpu as pltpu

F32 = jnp.float32
BF16 = jnp.bfloat16

D_MODEL = 1024
D_GMLP = 512
G_A = 4
CHUNK = 128
D_MLSTM = 512
H_B = 4
HV = 128
HK = 64
D_QK = 256
D_FF = 4096
DEPTH = 1
ALPHA = (2 * DEPTH) ** 0.25
LN_EPS = 1e-5
LOG2E = math.log2(math.e)

ADA_ROWS = 16
HALO = 16
SUB = 8
N_GATE = 4
Q_C, Q_CM, Q_B, Q_CMAX, Q_BTOT, N_GQ = 0, 2, 4, 6, 8, 10
CT_DIR = 64
AUG = 16
HVA = HV + AUG
VMEM_LIMIT_BYTES = 56 * 1024 * 1024
PROJ_TILE = 2048
PROJ_SUB = 256
POST_TILE = 1024
POST_SUB = 256
POST_FF = 1024
OUT_GROUP = 16

NT_DIMS = (((1,), (1,)), ((), ()))


def _ln_stats(x):
    mu = jnp.mean(x, axis=-1, keepdims=True)
    xc = x - mu
    var = jnp.mean(xc * xc, axis=-1, keepdims=True)
    return xc * lax.rsqrt(var + LN_EPS)


def _sigmoid(x):
    return 1.0 / (1.0 + jnp.exp(-x))


def _const_spec(shape):
    nd = len(shape)
    return pl.BlockSpec(shape, lambda *_: (0,) * nd, pipeline_mode=pl.Buffered(1))


def _ada_kernel(c_ref, w_ref, b_ref, o_ref):
    c = c_ref[...]
    s = (c * _sigmoid(c)).astype(BF16)
    o_ref[...] = jnp.dot(s, w_ref[...].astype(BF16), preferred_element_type=F32) + b_ref[...]


def _ada(cond, w, b):
    m, d = cond.shape
    n = w.shape[1]
    bn = 1536
    return pl.pallas_call(
        _ada_kernel,
        grid=(n // bn,),
        in_specs=[pl.BlockSpec((m, d), lambda j: (0, 0)),
                  pl.BlockSpec((d, bn), lambda j: (0, j)),
                  pl.BlockSpec((1, bn), lambda j: (0, j))],
        out_specs=pl.BlockSpec((m, bn), lambda j: (0, j)),
        out_shape=jax.ShapeDtypeStruct((m, n), F32),
        name="ada",
    )(cond, w, b)


def _proj_kernel(x_ref, xp_ref, xn_ref, sh_ref, sc_ref, wuv_ref, wqk_ref, wvt_ref, wo_ref,
                 ws_ref, bs_ref, lnvg_ref, lnvb_ref, cw_ref, qs_ref, bg_ref, hng_ref,
                 ya_ref, kk_ref, qt_ref, vt_ref, og_ref, gr_ref):
    i = pl.program_id(1)
    nt = pl.num_programs(1)
    tm = x_ref.shape[0]
    ts = min(PROJ_SUB, tm)
    nsub = tm // ts
    sc1 = 1.0 + sc_ref[...]
    sh = sh_ref[...]
    cw = cw_ref[...]
    lane = lax.broadcasted_iota(jnp.int32, (ts, CHUNK), 1)

    def mod(xv):
        return _ln_stats(xv) * sc1 + sh

    def project(s, hm, hext):
        r0 = s * ts
        rows = slice(r0, r0 + ts)
        puv = jnp.dot(hm, wuv_ref[...], preferred_element_type=F32)
        pqk = jnp.dot(hext, wqk_ref[...], preferred_element_type=F32)
        vtg = lax.dot_general(wvt_ref[...], hm, NT_DIMS, preferred_element_type=F32)
        vt_ref[:, rows] = vtg[:D_MLSTM, :].astype(BF16)
        og_ref[rows, :] = (_sigmoid(jnp.dot(hm, wo_ref[...], preferred_element_type=F32)) * hng_ref[...]).astype(BF16)

        conv = (pltpu.roll(pqk, 1, 0) * cw[0:1, :] + pqk * cw[1:2, :]
                + pltpu.roll(pqk, ts + 2 * HALO - 1, 0) * cw[2:3, :])
        conv = conv[HALO:HALO + ts, :]
        act = conv * _sigmoid(conv) * qs_ref[...]
        for pair in range(H_B // 2):
            blk = act[:, D_QK + pair * CHUNK:D_QK + (pair + 1) * CHUNK]
            swp = pltpu.roll(blk, HK, 1)
            kk_ref[rows, (2 * pair) * CHUNK:(2 * pair + 1) * CHUNK] = jnp.where(lane < HK, blk, swp).astype(BF16)
            kk_ref[rows, (2 * pair + 1) * CHUNK:(2 * pair + 2) * CHUNK] = jnp.where(lane < HK, swp, blk).astype(BF16)
            for c in range(ts // CHUNK):
                blk = act[c * CHUNK:(c + 1) * CHUNK, pair * CHUNK:(pair + 1) * CHUNK]
                qt_ref[pair * CHUNK:(pair + 1) * CHUNK, r0 + c * CHUNK:r0 + (c + 1) * CHUNK] = blk.T.astype(BF16)

        gt = vtg[D_MLSTM:, :] + bg_ref[...]
        for h in range(H_B):
            for q in range(N_GATE):
                for c in range(ts // CHUNK):
                    jr = r0 // CHUNK + c
                    gr_ref[h, q, jr:jr + 1, :] = gt[h * N_GATE + q:h * N_GATE + q + 1, c * CHUNK:(c + 1) * CHUNK]

        lnv = (_ln_stats(puv[:, D_GMLP:]) * lnvg_ref[...] + lnvb_ref[...]).astype(BF16)
        nch = ts // CHUNK
        for g in range(G_A):
            cols = slice(g * CHUNK, (g + 1) * CHUNK)
            side = jnp.concatenate([lnv[c * CHUNK:(c + 1) * CHUNK, cols] for c in range(nch)], axis=1)
            mixed = jnp.dot(ws_ref[g], side, preferred_element_type=F32)
            for c in range(nch):
                m_c = mixed[:, c * CHUNK:(c + 1) * CHUNK] + bs_ref[g]
                ya_ref[r0 + c * CHUNK:r0 + (c + 1) * CHUNK, cols] = (
                    puv[c * CHUNK:(c + 1) * CHUNK, cols] * m_c).astype(BF16)

    if gr_ref.shape[2] > tm // CHUNK:
        gr_ref[:, :, tm // CHUNK:, :] = jnp.zeros((H_B, N_GATE, gr_ref.shape[2] - tm // CHUNK, CHUNK), F32)
    keep_prev = jnp.where(i > 0, 1.0, 0.0)
    keep_next = jnp.where(i < nt - 1, 1.0, 0.0)
    left = (mod(xp_ref[...]) * keep_prev).astype(BF16)
    cur = mod(x_ref[0:ts, :]).astype(BF16)
    for s in range(nsub):
        if s + 1 < nsub:
            nxt = mod(x_ref[(s + 1) * ts:(s + 2) * ts, :]).astype(BF16)
        else:
            nxt = (mod(xn_ref[...]) * keep_next).astype(BF16)
        project(s, cur, jnp.concatenate([left, cur, nxt[:HALO, :]], axis=0))
        left, cur = cur[ts - HALO:, :], nxt


def _mod_spec(k, row=None):
    return pl.BlockSpec((None, 1, D_MODEL), lambda b, i: (b if row is None else row, 0, k))


def _proj(x, mods, mod_row, wts, tm):
    bsz, s, d = x.shape
    nt = s // tm
    hb = tm // HALO
    nhb = s // HALO
    (wuv, wqk, wvt, wo, ws, bs, lnvg, lnvb, cw, qs, bg, hng) = wts
    bg_t = jnp.broadcast_to(bg, (bg.shape[0], min(PROJ_SUB, tm)))
    tok = lambda n: jax.ShapeDtypeStruct((bsz, s, n), BF16)
    tok_spec = lambda n: pl.BlockSpec((None, tm, n), lambda b, i: (b, i, 0))
    tr = lambda n: jax.ShapeDtypeStruct((bsz, n, s), BF16)
    tr_spec = lambda n: pl.BlockSpec((None, n, tm), lambda b, i: (b, 0, i))
    return pl.pallas_call(
        _proj_kernel,
        grid=(bsz, nt),
        in_specs=[
            pl.BlockSpec((None, tm, d), lambda b, i: (b, i, 0)),
            pl.BlockSpec((None, HALO, d), lambda b, i: (b, jnp.maximum(i * hb - 1, 0), 0)),
            pl.BlockSpec((None, HALO, d), lambda b, i: (b, jnp.minimum((i + 1) * hb, nhb - 1), 0)),
            _mod_spec(0, mod_row), _mod_spec(1, mod_row),
            _const_spec(wuv.shape), _const_spec(wqk.shape), _const_spec(wvt.shape), _const_spec(wo.shape),
            _const_spec(ws.shape), _const_spec(bs.shape), _const_spec(lnvg.shape), _const_spec(lnvb.shape),
            _const_spec(cw.shape), _const_spec(qs.shape), _const_spec(bg_t.shape), _const_spec(hng.shape),
        ],
        out_specs=[tok_spec(D_GMLP), tok_spec(2 * D_QK), tr_spec(D_QK), tr_spec(D_MLSTM), tok_spec(D_MLSTM),
                   pl.BlockSpec((None, H_B, N_GATE, max(tm // CHUNK, SUB), CHUNK), lambda b, i: (b, 0, 0, i, 0))],
        out_shape=[tok(D_GMLP), tok(2 * D_QK), tr(D_QK), tr(D_MLSTM), tok(D_MLSTM),
                   jax.ShapeDtypeStruct((bsz, H_B, N_GATE, max(s // CHUNK, SUB), CHUNK), F32)],
        compiler_params=pltpu.CompilerParams(
            dimension_semantics=("parallel", "arbitrary"), vmem_limit_bytes=VMEM_LIMIT_BYTES),
        name="proj",
    )(x, x, x, mods, mods, wuv, wqk, wvt, wo, ws, bs, lnvg, lnvb, cw, qs, bg_t, hng)


def _prefix(x, op, lane, fill):
    d = 1
    while d < CHUNK:
        x = op(x, jnp.where(lane >= d, pltpu.roll(x, d, 1), fill))
        d *= 2
    return x


def _suffix(x, op, lane, fill):
    d = 1
    while d < CHUNK:
        x = op(x, jnp.where(lane < CHUNK - d, pltpu.roll(x, CHUNK - d, 1), fill))
        d *= 2
    return x


def _gates_kernel(gr_ref, grc_ref, gq_ref, gqc_ref, ct_ref):
    for src, dst in ((gr_ref, gq_ref), (grc_ref, gqc_ref)):
        nrow = src.shape[2]
        shape = (H_B * nrow, CHUNK)
        lane = lax.broadcasted_iota(jnp.int32, shape, 1)
        cs = []
        for r, scan in enumerate((_prefix, _suffix)):
            gi = src[:, r, :, :].reshape(shape)
            gf = src[:, 2 + r, :, :].reshape(shape)
            lf = (jnp.minimum(gf, 0.0) - jnp.log(1.0 + jnp.exp(-jnp.abs(gf)))) * LOG2E
            b = scan(lf, jnp.add, lane, 0.0)
            c = gi * LOG2E - b
            cs.append(c)
            planes = ((Q_C, c), (Q_CM, scan(c, jnp.maximum, lane, -jnp.inf)), (Q_B, b),
                      (Q_CMAX, jnp.broadcast_to(jnp.max(c, axis=1, keepdims=True), shape)),
                      (Q_BTOT, jnp.broadcast_to(jnp.min(b, axis=1, keepdims=True), shape)))
            for q, val in planes:
                dst[:, q + r, :, :] = val.reshape(H_B, nrow, CHUNK)
        if src is gr_ref:
            zrows = jnp.zeros((CT_DIR - nrow, CHUNK), F32)
            for h in range(H_B):
                tile = jnp.concatenate([cs[0][h * nrow:(h + 1) * nrow, :], zrows,
                                        cs[1][h * nrow:(h + 1) * nrow, :], zrows], axis=0)
                ct_ref[h] = tile.T


def _gates(gr, grc):
    bsz = gr.shape[0]
    assert gr.shape[3] <= CT_DIR
    spec = lambda a, n: pl.BlockSpec((None, H_B, n, a.shape[3], CHUNK), lambda b: (b, 0, 0, 0, 0))
    shape = lambda a: jax.ShapeDtypeStruct((bsz, H_B, N_GQ, a.shape[3], CHUNK), F32)
    return pl.pallas_call(
        _gates_kernel,
        grid=(bsz,),
        in_specs=[spec(gr, N_GATE), spec(grc, N_GATE)],
        out_specs=[spec(gr, N_GQ), spec(grc, N_GQ),
                   pl.BlockSpec((None, H_B, CHUNK, CHUNK), lambda b: (b, 0, 0, 0))],
        out_shape=[shape(gr), shape(grc), jax.ShapeDtypeStruct((bsz, H_B, CHUNK, CHUNK), F32)],
        name="gates",
    )(gr, grc)


def _mlstm_kernel(kk_ref, qt_ref, vt_ref, og_ref, gr_ref, ct_ref, kkc_ref, vtc_ref, grc_ref, yb_ref,
                  m0_s, dc_s, stf_s, stb_s, rhs_s):
    ncx = kkc_ref.shape[0] // CHUNK
    ncl = kk_ref.shape[0] // CHUNK
    aug_rows = jnp.where(lax.broadcasted_iota(jnp.int32, (AUG, CHUNK), 0) == 0, 1.0, 0.0).astype(BF16)
    lane1 = lax.broadcasted_iota(jnp.int32, (1, CHUNK), 1)
    lane_st = lax.broadcasted_iota(jnp.int32, (HVA, CHUNK), 1)

    def ctx_chunk(jc):
        return jc, (lambda q: grc_ref[q, jc:jc + 1, :]), None

    def lat_chunk(jl):
        return ncx + jl, (lambda q: gr_ref[q, pl.ds(jl, 1), :]), jl

    def delta_state(kk, vt, chunk):
        j, vec, _ = chunk
        vta = jnp.concatenate([vt, aug_rows], axis=0)
        lhs = jnp.concatenate([vta * jnp.exp2(vec(Q_C + r) - vec(Q_CMAX + r)).astype(BF16)
                               for r in range(2)], axis=0)
        d2 = jnp.dot(lhs, kk, preferred_element_type=F32)
        dc_s[j] = jnp.where(lane_st < HK, d2[:HVA, :], d2[HVA:, :])

    for jc in range(ncx):
        sl = slice(jc * CHUNK, (jc + 1) * CHUNK)
        delta_state(kkc_ref[sl, :], vtc_ref[:, sl], ctx_chunk(jc))

    def prep_body(jl, carry):
        t0 = pl.multiple_of(jl * CHUNK, CHUNK)
        delta_state(kk_ref[pl.ds(t0, CHUNK), :], vt_ref[:, pl.ds(t0, CHUNK)], lat_chunk(jl))
        return carry

    lax.fori_loop(0, ncl, prep_body, 0, unroll=8)

    def scan_step(chunks, state, ms):
        stf_s[chunks[0][0]] = state.astype(BF16)
        stb_s[chunks[1][0]] = state.astype(BF16)
        coef, new_ms = [], []
        for r, (_, vec, jl) in enumerate(chunks):
            if jl is not None:
                m0_s[r, pl.ds(jl, 1), :] = ms[r]
            cmax = vec(Q_CMAX + r)
            g = jnp.maximum(ms[r], cmax)
            coef.append((jnp.exp2(ms[r] - g), jnp.exp2(cmax - g)))
            new_ms.append(vec(Q_BTOT + r) + g)
        a_l = jnp.where(lane1 < HK, coef[0][0], coef[1][0])
        b_l = jnp.where(lane1 < HK, coef[0][1], coef[1][1])
        dc = jnp.where(lane_st < HK, dc_s[chunks[0][0]], dc_s[chunks[1][0]])
        return a_l * state + b_l * dc, tuple(new_ms)

    state = jnp.zeros((HVA, CHUNK), F32)
    ms = (jnp.zeros((1, CHUNK), F32), jnp.zeros((1, CHUNK), F32))
    for p in range(ncx):
        state, ms = scan_step((ctx_chunk(p), ctx_chunk(ncx - 1 - p)), state, ms)

    def scan_body(p, carry):
        return scan_step((lat_chunk(p), lat_chunk(ncl - 1 - p)), *carry)

    lax.fori_loop(0, ncl, scan_body, (state, ms), unroll=2)

    si = lax.broadcasted_iota(jnp.int32, (CHUNK, CHUNK), 0)
    ti = lax.broadcasted_iota(jnp.int32, (CHUNK, CHUNK), 1)
    masks = (si <= ti, si >= ti)
    zq = jnp.zeros((HK, CHUNK), BF16)

    def stabiliser(r, jl):
        m0 = m0_s[r, pl.ds(jl, 1), :]
        return m0, jnp.maximum(m0, gr_ref[Q_CM + r, pl.ds(jl, 1), :])

    def weights_stage(jl):
        t0 = jl * CHUNK
        qt = qt_ref[:, pl.ds(t0, CHUNK)]
        st = lax.dot_general(kk_ref[pl.ds(t0, CHUNK), :], jnp.concatenate([qt, zq], axis=0),
                             (((1,), (0,)), ((), ())), preferred_element_type=F32)
        qtf = qt.astype(F32)
        for r in range(2):
            m0, g = stabiliser(r, jl)
            ccol = ct_ref[:, CT_DIR * r + jl:CT_DIR * r + jl + 1]
            w = jnp.where(masks[r], jnp.exp2(ccol - g), 0.0)
            rhs_s[jl, 0:CHUNK, r * CHUNK:(r + 1) * CHUNK] = (st * w).astype(BF16)
            qw = (qtf * jnp.exp2(m0 - g)).astype(BF16)
            rhs_s[jl, CHUNK + r * HK:CHUNK + (r + 1) * HK, r * CHUNK:(r + 1) * CHUNK] = qw
            rhs_s[jl, CHUNK + (1 - r) * HK:CHUNK + (2 - r) * HK, r * CHUNK:(r + 1) * CHUNK] = zq

    def output_stage(jl):
        j = ncx + jl
        t0 = jl * CHUNK
        vta = jnp.concatenate([vt_ref[:, pl.ds(t0, CHUNK)], aug_rows], axis=0)
        stt = jnp.where(lane_st < HK, stf_s[j], stb_s[j])
        nd = jnp.dot(jnp.concatenate([vta, stt], axis=1), rhs_s[jl], preferred_element_type=F32)
        ht = None
        for r in range(2):
            blk = nd[:, r * CHUNK:(r + 1) * CHUNK]
            fl = jnp.exp2(-(gr_ref[Q_B + r, pl.ds(jl, 1), :] + stabiliser(r, jl)[1]))
            rden = 1.0 / jnp.maximum(jnp.abs(blk[HV:HV + 1, :]), fl)
            part = blk[:HV, :] * rden
            ht = part if ht is None else ht + part
        mu = jnp.mean(ht, axis=0, keepdims=True)
        xc = ht - mu
        var = jnp.mean(xc * xc, axis=0, keepdims=True)
        y = (xc * lax.rsqrt(var + LN_EPS)).T
        yb_ref[pl.ds(t0, CHUNK), :] = y.astype(BF16) * og_ref[pl.ds(t0, CHUNK), :]

    for u in range(OUT_GROUP):
        weights_stage(u)
    for i in range(ncl // OUT_GROUP - 1):
        for u in range(OUT_GROUP):
            output_stage(i * OUT_GROUP + u)
        for u in range(OUT_GROUP):
            weights_stage((i + 1) * OUT_GROUP + u)
    for u in range(OUT_GROUP):
        output_stage(ncl - OUT_GROUP + u)


def _mlstm(kk, qt, vt, og, gr, ct, kkc, vtc, grc):
    bsz, s, _ = kk.shape
    sc = kkc.shape[1]
    ncl = s // CHUNK
    nr = sc // CHUNK + ncl
    head = lambda n: pl.BlockSpec((None, n, CHUNK), lambda b, h: (b, 0, h))
    head_t = lambda rows, n: pl.BlockSpec((None, rows, n), lambda b, h: (b, h, 0))
    gates = lambda a: pl.BlockSpec((None, None, N_GQ, a.shape[3], CHUNK), lambda b, h: (b, h, 0, 0, 0))
    return pl.pallas_call(
        _mlstm_kernel,
        grid=(bsz, H_B),
        in_specs=[head(s), head_t(HK, s), head_t(HV, s), head(s), gates(gr),
                  pl.BlockSpec((None, None, CHUNK, CHUNK), lambda b, h: (b, h, 0, 0)),
                  head(sc), head_t(HV, sc), gates(grc)],
        out_specs=head(s),
        out_shape=jax.ShapeDtypeStruct((bsz, s, D_MLSTM), BF16),
        scratch_shapes=[pltpu.VMEM((2, ncl, CHUNK), F32),
                        pltpu.VMEM((nr, HVA, CHUNK), F32),
                        pltpu.VMEM((nr, HVA, CHUNK), BF16), pltpu.VMEM((nr, HVA, CHUNK), BF16),
                        pltpu.VMEM((ncl, 2 * CHUNK, 2 * CHUNK), BF16)],
        compiler_params=pltpu.CompilerParams(
            dimension_semantics=("parallel", "arbitrary"), vmem_limit_bytes=VMEM_LIMIT_BYTES),
        name="mlstm",
    )(kk, qt, vt, og, gr, ct, kkc, vtc, grc)


def _post_kernel(x_ref, ya_ref, yb_ref, g1_ref, sh2_ref, sc2_ref, g2_ref, wo_ref, l1g_ref, l1b_ref,
                 w1_ref, b1_ref, w2_ref, b2_ref, l2g_ref, l2b_ref, out_ref):
    nf = POST_FF
    nparts = D_FF // nf
    bounds = list(range(0, x_ref.shape[0] + 1, POST_SUB))
    nsub = len(bounds) - 1

    def head(r):
        rows = slice(bounds[r], bounds[r + 1])
        yab = jnp.concatenate([ya_ref[rows, :], yb_ref[rows, :]], axis=1)
        y = jnp.dot(yab, wo_ref[...], preferred_element_type=F32)
        x1 = _ln_stats(ALPHA * x_ref[rows, :] + g1_ref[...] * y) * l1g_ref[...] + l1b_ref[...]
        h2 = (_ln_stats(x1) * (1.0 + sc2_ref[...]) + sh2_ref[...]).astype(BF16)
        return x1, h2

    def up(h2, f):
        cols = slice(f * nf, (f + 1) * nf)
        return jnp.dot(h2, w1_ref[:, cols], preferred_element_type=F32) + b1_ref[:, cols]

    def down(t, f):
        t = jnp.maximum(t, 0.0)
        return jnp.dot((t * t).astype(BF16), w2_ref[f * nf:(f + 1) * nf, :], preferred_element_type=F32)

    def tail(r, x1, z):
        rows = slice(bounds[r], bounds[r + 1])
        out_ref[rows, :] = _ln_stats(ALPHA * x1 + g2_ref[...] * (z + b2_ref[...])) * l2g_ref[...] + l2b_ref[...]

    cur = head(0)
    pending = None
    for r in range(nsub):
        x1, h2 = cur
        t = up(h2, 0)
        z = None
        for f in range(nparts):
            t_next = up(h2, f + 1) if f + 1 < nparts else None
            if f == min(1, nparts - 1) and pending is not None:
                tail(*pending)
            if f == min(2, nparts - 1) and r + 1 < nsub:
                cur = head(r + 1)
            d = down(t, f)
            z = d if z is None else z + d
            t = t_next
        pending = (r, x1, z)
    tail(*pending)


def _post(x, ya, yb, mods, wts, tm):
    bsz, s, d = x.shape
    (wo, l1g, l1b, w1, b1, w2, b2, l2g, l2b) = wts
    tok_spec = lambda n: pl.BlockSpec((None, tm, n), lambda b, i: (b, i, 0))
    return pl.pallas_call(
        _post_kernel,
        grid=(bsz, s // tm),
        in_specs=[tok_spec(d), tok_spec(D_GMLP), tok_spec(D_MLSTM),
                  _mod_spec(2), _mod_spec(3), _mod_spec(4), _mod_spec(5)]
                 + [_const_spec(w.shape) for w in wts],
        out_specs=tok_spec(d),
        out_shape=jax.ShapeDtypeStruct((bsz, s, d), F32),
        compiler_params=pltpu.CompilerParams(
            dimension_semantics=("parallel", "arbitrary"), vmem_limit_bytes=VMEM_LIMIT_BYTES),
        name="post",
    )(x, ya, yb, mods, mods, mods, mods, wo, l1g, l1b, w1, b1, w2, b2, l2g, l2b)


def _proj_weights(w_in, w_s, b_s, ln_v_g, ln_v_b, conv_qk, b_gates, hn_g):
    q0 = 2 * D_GMLP
    v0 = q0 + 2 * D_QK
    o0 = v0 + D_MLSTM
    g0 = o0 + D_MLSTM
    def gate_tiles(a):
        return a.reshape(2, 2, H_B, -1).transpose(2, 1, 0, 3).reshape(H_B * N_GATE, -1)

    wuv = w_in[:, :q0].astype(BF16)
    wqk = w_in[:, q0:v0].astype(BF16)
    wvt = jnp.concatenate([w_in[:, v0:o0].T, gate_tiles(w_in[:, g0:].T)], axis=0).astype(BF16)
    wo = w_in[:, o0:g0].astype(BF16)
    ws = w_s.astype(BF16)
    bs = jnp.broadcast_to(b_s[:, :, None], (G_A, CHUNK, CHUNK))
    qs = jnp.concatenate([jnp.full((D_QK,), HK ** -0.5, F32), jnp.ones((D_QK,), F32)])[None, :]
    return (wuv, wqk, wvt, wo, ws, bs, ln_v_g[None, :], ln_v_b[None, :], conv_qk, qs, gate_tiles(b_gates[:, None]),
            hn_g[None, :])


def kernel(x, c, ctx, c_ctx, w_ada, b_ada, w_in, w_s, b_s, ln_v_g, ln_v_b, conv_qk, b_gates, hn_g,
           w_out, ln1_g, ln1_b, w1, b1, w2, b2, ln2_g, ln2_b):
    bsz = x.shape[0]
    l = 0
    cond = jnp.concatenate([c, c_ctx[None, :], jnp.zeros((ADA_ROWS - bsz - 1, D_MODEL), F32)], axis=0)
    mods = _ada(cond, w_ada[l], b_ada[l][None, :])[:, None, :]

    pw = _proj_weights(w_in[l], w_s[l], b_s[l], ln_v_g[l], ln_v_b[l], conv_qk[l], b_gates[l], hn_g[l])
    ya, kk, qt, vt, og, gr = _proj(x, mods, None, pw, PROJ_TILE)
    _, kkc, _, vtc, _, grc = _proj(ctx, mods, bsz, pw, ctx.shape[1])

    gq, gqc, ct = _gates(gr, grc)
    yb = _mlstm(kk, qt, vt, og, gq, ct, kkc, vtc, gqc)

    row = lambda v: v[None, :]
    post_w = (w_out[l].astype(BF16), row(ln1_g[l]), row(ln1_b[l]), w1[l].astype(BF16), row(b1[l]),
              w2[l].astype(BF16), row(b2[l]), row(ln2_g[l]), row(ln2_b[l]))
    return _post(x, ya, yb, mods, post_w, POST_TILE)
```

```python
import math

import jax
import jax.numpy as jnp
from jax import lax
from jax.experimental import pallas as pl
from jax.experimental.pallas import tpu as pltpu

F32 = jnp.float32
BF16 = jnp.bfloat16

D_MODEL = 1024
D_GMLP = 512
G_A = 4
CHUNK = 128
D_MLSTM = 512
H_B = 4
HV = 128
HK = 64
D_QK = 256
D_FF = 4096
DEPTH = 1
ALPHA = (2 * DEPTH) ** 0.25
LN_EPS = 1e-5
LOG2E = math.log2(math.e)

ADA_ROWS = 16
HALO = 16
SUB = 8
N_GATE = 4
Q_C, Q_CM, Q_B, Q_CMAX, Q_BTOT, N_GQ = 0, 2, 4, 6, 8, 10
CT_DIR = 64
AUG = 16
HVA = HV + AUG
VMEM_LIMIT_BYTES = 56 * 1024 * 1024
PROJ_TILE = 2048
PROJ_SUB = 256
POST_TILE = 1024
POST_SUB = 256
POST_FF = 1024
OUT_GROUP = 16

NT_DIMS = (((1,), (1,)), ((), ()))


def _ln_stats(x):
    mu = jnp.mean(x, axis=-1, keepdims=True)
    xc = x - mu
    var = jnp.mean(xc * xc, axis=-1, keepdims=True)
    return xc * lax.rsqrt(var + LN_EPS)


def _sigmoid(x):
    return 1.0 / (1.0 + jnp.exp(-x))


def _const_spec(shape):
    nd = len(shape)
    return pl.BlockSpec(shape, lambda *_: (0,) * nd, pipeline_mode=pl.Buffered(1))


def _ada_kernel(c_ref, w_ref, b_ref, o_ref):
    c = c_ref[...]
    s = (c * _sigmoid(c)).astype(BF16)
    o_ref[...] = jnp.dot(s, w_ref[...].astype(BF16), preferred_element_type=F32) + b_ref[...]


def _ada(cond, w, b):
    m, d = cond.shape
    n = w.shape[1]
    bn = 1536
    return pl.pallas_call(
        _ada_kernel,
        grid=(n // bn,),
        in_specs=[pl.BlockSpec((m, d), lambda j: (0, 0)),
                  pl.BlockSpec((d, bn), lambda j: (0, j)),
                  pl.BlockSpec((1, bn), lambda j: (0, j))],
        out_specs=pl.BlockSpec((m, bn), lambda j: (0, j)),
        out_shape=jax.ShapeDtypeStruct((m, n), F32),
        name="ada",
    )(cond, w, b)


def _proj_kernel(x_ref, xp_ref, xn_ref, sh_ref, sc_ref, wuv_ref, wqk_ref, wvt_ref, wo_ref,
                 ws_ref, bs_ref, lnvg_ref, lnvb_ref, cw_ref, qs_ref, bg_ref, hng_ref,
                 ya_ref, kk_ref, qt_ref, vt_ref, og_ref, gr_ref):
    i = pl.program_id(1)
    nt = pl.num_programs(1)
    tm = x_ref.shape[0]
    ts = min(PROJ_SUB, tm)
    nsub = tm // ts
    sc1 = 1.0 + sc_ref[...]
    sh = sh_ref[...]
    cw = cw_ref[...]
    lane = lax.broadcasted_iota(jnp.int32, (ts, CHUNK), 1)

    def mod(xv):
        return _ln_stats(xv) * sc1 + sh

    def project(s, hm, hext):
        r0 = s * ts
        rows = slice(r0, r0 + ts)
        puv = jnp.dot(hm, wuv_ref[...], preferred_element_type=F32)
        pqk = jnp.dot(hext, wqk_ref[...], preferred_element_type=F32)
        vtg = lax.dot_general(wvt_ref[...], hm, NT_DIMS, preferred_element_type=F32)
        vt_ref[:, rows] = vtg[:D_MLSTM, :].astype(BF16)
        og_ref[rows, :] = (_sigmoid(jnp.dot(hm, wo_ref[...], preferred_element_type=F32)) * hng_ref[...]).astype(BF16)

        conv = (pltpu.roll(pqk, 1, 0) * cw[0:1, :] + pqk * cw[1:2, :]
                + pltpu.roll(pqk, ts + 2 * HALO - 1, 0) * cw[2:3, :])
        conv = conv[HALO:HALO + ts, :]
        act = conv * _sigmoid(conv) * qs_ref[...]
        for pair in range(H_B // 2):
            blk = act[:, D_QK + pair * CHUNK:D_QK + (pair + 1) * CHUNK]
            swp = pltpu.roll(blk, HK, 1)
            kk_ref[rows, (2 * pair) * CHUNK:(2 * pair + 1) * CHUNK] = jnp.where(lane < HK, blk, swp).astype(BF16)
            kk_ref[rows, (2 * pair + 1) * CHUNK:(2 * pair + 2) * CHUNK] = jnp.where(lane < HK, swp, blk).astype(BF16)
            for c in range(ts // CHUNK):
                blk = act[c * CHUNK:(c + 1) * CHUNK, pair * CHUNK:(pair + 1) * CHUNK]
                qt_ref[pair * CHUNK:(pair + 1) * CHUNK, r0 + c * CHUNK:r0 + (c + 1) * CHUNK] = blk.T.astype(BF16)

        gt = vtg[D_MLSTM:, :] + bg_ref[...]
        for h in range(H_B):
            for q in range(N_GATE):
                for c in range(ts // CHUNK):
                    jr = r0 // CHUNK + c
                    gr_ref[h, q, jr:jr + 1, :] = gt[h * N_GATE + q:h * N_GATE + q + 1, c * CHUNK:(c + 1) * CHUNK]

        lnv = (_ln_stats(puv[:, D_GMLP:]) * lnvg_ref[...] + lnvb_ref[...]).astype(BF16)
        nch = ts // CHUNK
        for g in range(G_A):
            cols = slice(g * CHUNK, (g + 1) * CHUNK)
            side = jnp.concatenate([lnv[c * CHUNK:(c + 1) * CHUNK, cols] for c in range(nch)], axis=1)
            mixed = jnp.dot(ws_ref[g], side, preferred_element_type=F32)
            for c in range(nch):
                m_c = mixed[:, c * CHUNK:(c + 1) * CHUNK] + bs_ref[g]
                ya_ref[r0 + c * CHUNK:r0 + (c + 1) * CHUNK, cols] = (
                    puv[c * CHUNK:(c + 1) * CHUNK, cols] * m_c).astype(BF16)

    if gr_ref.shape[2] > tm // CHUNK:
        gr_ref[:, :, tm // CHUNK:, :] = jnp.zeros((H_B, N_GATE, gr_ref.shape[2] - tm // CHUNK, CHUNK), F32)
    keep_prev = jnp.where(i > 0, 1.0, 0.0)
    keep_next = jnp.where(i < nt - 1, 1.0, 0.0)
    left = (mod(xp_ref[...]) * keep_prev).astype(BF16)
    cur = mod(x_ref[0:ts, :]).astype(BF16)
    for s in range(nsub):
        if s + 1 < nsub:
            nxt = mod(x_ref[(s + 1) * ts:(s + 2) * ts, :]).astype(BF16)
        else:
            nxt = (mod(xn_ref[...]) * keep_next).astype(BF16)
        project(s, cur, jnp.concatenate([left, cur, nxt[:HALO, :]], axis=0))
        left, cur = cur[ts - HALO:, :], nxt


def _mod_spec(k, row=None):
    return pl.BlockSpec((None, 1, D_MODEL), lambda b, i: (b if row is None else row, 0, k))


def _proj(x, mods, mod_row, wts, tm):
    bsz, s, d = x.shape
    nt = s // tm
    hb = tm // HALO
    nhb = s // HALO
    (wuv, wqk, wvt, wo, ws, bs, lnvg, lnvb, cw, qs, bg, hng) = wts
    bg_t = jnp.broadcast_to(bg, (bg.shape[0], min(PROJ_SUB, tm)))
    tok = lambda n: jax.ShapeDtypeStruct((bsz, s, n), BF16)
    tok_spec = lambda n: pl.BlockSpec((None, tm, n), lambda b, i: (b, i, 0))
    tr = lambda n: jax.ShapeDtypeStruct((bsz, n, s), BF16)
    tr_spec = lambda n: pl.BlockSpec((None, n, tm), lambda b, i: (b, 0, i))
    return pl.pallas_call(
        _proj_kernel,
        grid=(bsz, nt),
        in_specs=[
            pl.BlockSpec((None, tm, d), lambda b, i: (b, i, 0)),
            pl.BlockSpec((None, HALO, d), lambda b, i: (b, jnp.maximum(i * hb - 1, 0), 0)),
            pl.BlockSpec((None, HALO, d), lambda b, i: (b, jnp.minimum((i + 1) * hb, nhb - 1), 0)),
            _mod_spec(0, mod_row), _mod_spec(1, mod_row),
            _const_spec(wuv.shape), _const_spec(wqk.shape), _const_spec(wvt.shape), _const_spec(wo.shape),
            _const_spec(ws.shape), _const_spec(bs.shape), _const_spec(lnvg.shape), _const_spec(lnvb.shape),
            _const_spec(cw.shape), _const_spec(qs.shape), _const_spec(bg_t.shape), _const_spec(hng.shape),
        ],
        out_specs=[tok_spec(D_GMLP), tok_spec(2 * D_QK), tr_spec(D_QK), tr_spec(D_MLSTM), tok_spec(D_MLSTM),
                   pl.BlockSpec((None, H_B, N_GATE, max(tm // CHUNK, SUB), CHUNK), lambda b, i: (b, 0, 0, i, 0))],
        out_shape=[tok(D_GMLP), tok(2 * D_QK), tr(D_QK), tr(D_MLSTM), tok(D_MLSTM),
                   jax.ShapeDtypeStruct((bsz, H_B, N_GATE, max(s // CHUNK, SUB), CHUNK), F32)],
        compiler_params=pltpu.CompilerParams(
            dimension_semantics=("parallel", "arbitrary"), vmem_limit_bytes=VMEM_LIMIT_BYTES),
        name="proj",
    )(x, x, x, mods, mods, wuv, wqk, wvt, wo, ws, bs, lnvg, lnvb, cw, qs, bg_t, hng)


def _prefix(x, op, lane, fill):
    d = 1
    while d < CHUNK:
        x = op(x, jnp.where(lane >= d, pltpu.roll(x, d, 1), fill))
        d *= 2
    return x


def _suffix(x, op, lane, fill):
    d = 1
    while d < CHUNK:
        x = op(x, jnp.where(lane < CHUNK - d, pltpu.roll(x, CHUNK - d, 1), fill))
        d *= 2
    return x


def _gates_kernel(gr_ref, grc_ref, gq_ref, gqc_ref, ct_ref):
    for src, dst in ((gr_ref, gq_ref), (grc_ref, gqc_ref)):
        nrow = src.shape[2]
        shape = (H_B * nrow, CHUNK)
        lane = lax.broadcasted_iota(jnp.int32, shape, 1)
        cs = []
        for r, scan in enumerate((_prefix, _suffix)):
            gi = src[:, r, :, :].reshape(shape)
            gf = src[:, 2 + r, :, :].reshape(shape)
            lf = (jnp.minimum(gf, 0.0) - jnp.log(1.0 + jnp.exp(-jnp.abs(gf)))) * LOG2E
            b = scan(lf, jnp.add, lane, 0.0)
            c = gi * LOG2E - b
            cs.append(c)
            planes = ((Q_C, c), (Q_CM, scan(c, jnp.maximum, lane, -jnp.inf)), (Q_B, b),
                      (Q_CMAX, jnp.broadcast_to(jnp.max(c, axis=1, keepdims=True), shape)),
                      (Q_BTOT, jnp.broadcast_to(jnp.min(b, axis=1, keepdims=True), shape)))
            for q, val in planes:
                dst[:, q + r, :, :] = val.reshape(H_B, nrow, CHUNK)
        if src is gr_ref:
            zrows = jnp.zeros((CT_DIR - nrow, CHUNK), F32)
            for h in range(H_B):
                tile = jnp.concatenate([cs[0][h * nrow:(h + 1) * nrow, :], zrows,
                                        cs[1][h * nrow:(h + 1) * nrow, :], zrows], axis=0)
                ct_ref[h] = tile.T


def _gates(gr, grc):
    bsz = gr.shape[0]
    assert gr.shape[3] <= CT_DIR
    spec = lambda a, n: pl.BlockSpec((None, H_B, n, a.shape[3], CHUNK), lambda b: (b, 0, 0, 0, 0))
    shape = lambda a: jax.ShapeDtypeStruct((bsz, H_B, N_GQ, a.shape[3], CHUNK), F32)
    return pl.pallas_call(
        _gates_kernel,
        grid=(bsz,),
        in_specs=[spec(gr, N_GATE), spec(grc, N_GATE)],
        out_specs=[spec(gr, N_GQ), spec(grc, N_GQ),
                   pl.BlockSpec((None, H_B, CHUNK, CHUNK), lambda b: (b, 0, 0, 0))],
        out_shape=[shape(gr), shape(grc), jax.ShapeDtypeStruct((bsz, H_B, CHUNK, CHUNK), F32)],
        name="gates",
    )(gr, grc)


def _mlstm_kernel(kk_ref, qt_ref, vt_ref, og_ref, gr_ref, ct_ref, kkc_ref, vtc_ref, grc_ref, yb_ref,
                  m0_s, dc_s, stf_s, stb_s, rhs_s):
    ncx = kkc_ref.shape[0] // CHUNK
    ncl = kk_ref.shape[0] // CHUNK
    aug_rows = jnp.where(lax.broadcasted_iota(jnp.int32, (AUG, CHUNK), 0) == 0, 1.0, 0.0).astype(BF16)
    lane1 = lax.broadcasted_iota(jnp.int32, (1, CHUNK), 1)
    lane_st = lax.broadcasted_iota(jnp.int32, (HVA, CHUNK), 1)

    def ctx_chunk(jc):
        return jc, (lambda q: grc_ref[q, jc:jc + 1, :]), None

    def lat_chunk(jl):
        return ncx + jl, (lambda q: gr_ref[q, pl.ds(jl, 1), :]), jl

    def delta_state(kk, vt, chunk):
        j, vec, _ = chunk
        vta = jnp.concatenate([vt, aug_rows], axis=0)
        lhs = jnp.concatenate([vta * jnp.exp2(vec(Q_C + r) - vec(Q_CMAX + r)).astype(BF16)
                               for r in range(2)], axis=0)
        d2 = jnp.dot(lhs, kk, preferred_element_type=F32)
        dc_s[j] = jnp.where(lane_st < HK, d2[:HVA, :], d2[HVA:, :])

    for jc in range(ncx):
        sl = slice(jc * CHUNK, (jc + 1) * CHUNK)
        delta_state(kkc_ref[sl, :], vtc_ref[:, sl], ctx_chunk(jc))

    for jl in range(ncl):
        sl = slice(jl * CHUNK, (jl + 1) * CHUNK)
        delta_state(kk_ref[sl, :], vt_ref[:, sl], lat_chunk(jl))

    def scan_step(chunks, state, ms):
        stf_s[chunks[0][0]] = state.astype(BF16)
        stb_s[chunks[1][0]] = state.astype(BF16)
        coef, new_ms = [], []
        for r, (_, vec, jl) in enumerate(chunks):
            if jl is not None:
                m0_s[r, pl.ds(jl, 1), :] = ms[r]
            cmax = vec(Q_CMAX + r)
            g = jnp.maximum(ms[r], cmax)
            coef.append((jnp.exp2(ms[r] - g), jnp.exp2(cmax - g)))
            new_ms.append(vec(Q_BTOT + r) + g)
        a_l = jnp.where(lane1 < HK, coef[0][0], coef[1][0])
        b_l = jnp.where(lane1 < HK, coef[0][1], coef[1][1])
        dc = jnp.where(lane_st < HK, dc_s[chunks[0][0]], dc_s[chunks[1][0]])
        return a_l * state + b_l * dc, tuple(new_ms)

    state = jnp.zeros((HVA, CHUNK), F32)
    ms = (jnp.zeros((1, CHUNK), F32), jnp.zeros((1, CHUNK), F32))
    for p in range(ncx):
        state, ms = scan_step((ctx_chunk(p), ctx_chunk(ncx - 1 - p)), state, ms)

    for p in range(ncl):
        state, ms = scan_step((lat_chunk(p), lat_chunk(ncl - 1 - p)), state, ms)

    si = lax.broadcasted_iota(jnp.int32, (CHUNK, CHUNK), 0)
    ti = lax.broadcasted_iota(jnp.int32, (CHUNK, CHUNK), 1)
    masks = (si <= ti, si >= ti)
    zq = jnp.zeros((HK, CHUNK), BF16)

    def stabiliser(r, jl):
        m0 = m0_s[r, pl.ds(jl, 1), :]
        return m0, jnp.maximum(m0, gr_ref[Q_CM + r, pl.ds(jl, 1), :])

    def weights_stage(jl):
        t0 = jl * CHUNK
        qt = qt_ref[:, pl.ds(t0, CHUNK)]
        st = lax.dot_general(kk_ref[pl.ds(t0, CHUNK), :], jnp.concatenate([qt, zq], axis=0),
                             (((1,), (0,)), ((), ())), preferred_element_type=F32)
        qtf = qt.astype(F32)
        for r in range(2):
            m0, g = stabiliser(r, jl)
            ccol = ct_ref[:, CT_DIR * r + jl:CT_DIR * r + jl + 1]
            w = jnp.where(masks[r], jnp.exp2(ccol - g), 0.0)
            rhs_s[jl, 0:CHUNK, r * CHUNK:(r + 1) * CHUNK] = (st * w).astype(BF16)
            qw = (qtf * jnp.exp2(m0 - g)).astype(BF16)
            rhs_s[jl, CHUNK + r * HK:CHUNK + (r + 1) * HK, r * CHUNK:(r + 1) * CHUNK] = qw
            rhs_s[jl, CHUNK + (1 - r) * HK:CHUNK + (2 - r) * HK, r * CHUNK:(r + 1) * CHUNK] = zq

    def output_stage(jl):
        j = ncx + jl
        t0 = jl * CHUNK
        vta = jnp.concatenate([vt_ref[:, pl.ds(t0, CHUNK)], aug_rows], axis=0)
        stt = jnp.where(lane_st < HK, stf_s[j], stb_s[j])
        nd = jnp.dot(jnp.concatenate([vta, stt], axis=1), rhs_s[jl], preferred_element_type=F32)
        ht = None
        for r in range(2):
            blk = nd[:, r * CHUNK:(r + 1) * CHUNK]
            fl = jnp.exp2(-(gr_ref[Q_B + r, pl.ds(jl, 1), :] + stabiliser(r, jl)[1]))
            rden = 1.0 / jnp.maximum(jnp.abs(blk[HV:HV + 1, :]), fl)
            part = blk[:HV, :] * rden
            ht = part if ht is None else ht + part
        mu = jnp.mean(ht, axis=0, keepdims=True)
        xc = ht - mu
        var = jnp.mean(xc * xc, axis=0, keepdims=True)
        y = (xc * lax.rsqrt(var + LN_EPS)).T
        yb_ref[pl.ds(t0, CHUNK), :] = y.astype(BF16) * og_ref[pl.ds(t0, CHUNK), :]

    for u in range(OUT_GROUP):
        weights_stage(u)
    for i in range(ncl // OUT_GROUP - 1):
        for u in range(OUT_GROUP):
            output_stage(i * OUT_GROUP + u)
        for u in range(OUT_GROUP):
            weights_stage((i + 1) * OUT_GROUP + u)
    for u in range(OUT_GROUP):
        output_stage(ncl - OUT_GROUP + u)


def _mlstm(kk, qt, vt, og, gr, ct, kkc, vtc, grc):
    bsz, s, _ = kk.shape
    sc = kkc.shape[1]
    ncl = s // CHUNK
    nr = sc // CHUNK + ncl
    head = lambda n: pl.BlockSpec((None, n, CHUNK), lambda b, h: (b, 0, h))
    head_t = lambda rows, n: pl.BlockSpec((None, rows, n), lambda b, h: (b, h, 0))
    gates = lambda a: pl.BlockSpec((None, None, N_GQ, a.shape[3], CHUNK), lambda b, h: (b, h, 0, 0, 0))
    return pl.pallas_call(
        _mlstm_kernel,
        grid=(bsz, H_B),
        in_specs=[head(s), head_t(HK, s), head_t(HV, s), head(s), gates(gr),
                  pl.BlockSpec((None, None, CHUNK, CHUNK), lambda b, h: (b, h, 0, 0)),
                  head(sc), head_t(HV, sc), gates(grc)],
        out_specs=head(s),
        out_shape=jax.ShapeDtypeStruct((bsz, s, D_MLSTM), BF16),
        scratch_shapes=[pltpu.VMEM((2, ncl, CHUNK), F32),
                        pltpu.VMEM((nr, HVA, CHUNK), F32),
                        pltpu.VMEM((nr, HVA, CHUNK), BF16), pltpu.VMEM((nr, HVA, CHUNK), BF16),
                        pltpu.VMEM((ncl, 2 * CHUNK, 2 * CHUNK), BF16)],
        compiler_params=pltpu.CompilerParams(
            dimension_semantics=("parallel", "arbitrary"), vmem_limit_bytes=VMEM_LIMIT_BYTES),
        name="mlstm",
    )(kk, qt, vt, og, gr, ct, kkc, vtc, grc)


def _post_kernel(x_ref, ya_ref, yb_ref, g1_ref, sh2_ref, sc2_ref, g2_ref, wo_ref, l1g_ref, l1b_ref,
                 w1_ref, b1_ref, w2_ref, b2_ref, l2g_ref, l2b_ref, out_ref):
    nf = POST_FF
    nparts = D_FF // nf
    bounds = list(range(0, x_ref.shape[0] + 1, POST_SUB))
    nsub = len(bounds) - 1

    def head(r):
        rows = slice(bounds[r], bounds[r + 1])
        yab = jnp.concatenate([ya_ref[rows, :], yb_ref[rows, :]], axis=1)
        y = jnp.dot(yab, wo_ref[...], preferred_element_type=F32)
        x1 = _ln_stats(ALPHA * x_ref[rows, :] + g1_ref[...] * y) * l1g_ref[...] + l1b_ref[...]
        h2 = (_ln_stats(x1) * (1.0 + sc2_ref[...]) + sh2_ref[...]).astype(BF16)
        return x1, h2

    def up(h2, f):
        cols = slice(f * nf, (f + 1) * nf)
        return jnp.dot(h2, w1_ref[:, cols], preferred_element_type=F32) + b1_ref[:, cols]

    def down(t, f):
        t = jnp.maximum(t, 0.0)
        return jnp.dot((t * t).astype(BF16), w2_ref[f * nf:(f + 1) * nf, :], preferred_element_type=F32)

    def tail(r, x1, z):
        rows = slice(bounds[r], bounds[r + 1])
        out_ref[rows, :] = _ln_stats(ALPHA * x1 + g2_ref[...] * (z + b2_ref[...])) * l2g_ref[...] + l2b_ref[...]

    cur = head(0)
    pending = None
    for r in range(nsub):
        x1, h2 = cur
        t = up(h2, 0)
        z = None
        for f in range(nparts):
            t_next = up(h2, f + 1) if f + 1 < nparts else None
            if f == min(1, nparts - 1) and pending is not None:
                tail(*pending)
            if f == min(2, nparts - 1) and r + 1 < nsub:
                cur = head(r + 1)
            d = down(t, f)
            z = d if z is None else z + d
            t = t_next
        pending = (r, x1, z)
    tail(*pending)


def _post(x, ya, yb, mods, wts, tm):
    bsz, s, d = x.shape
    (wo, l1g, l1b, w1, b1, w2, b2, l2g, l2b) = wts
    tok_spec = lambda n: pl.BlockSpec((None, tm, n), lambda b, i: (b, i, 0))
    return pl.pallas_call(
        _post_kernel,
        grid=(bsz, s // tm),
        in_specs=[tok_spec(d), tok_spec(D_GMLP), tok_spec(D_MLSTM),
                  _mod_spec(2), _mod_spec(3), _mod_spec(4), _mod_spec(5)]
                 + [_const_spec(w.shape) for w in wts],
        out_specs=tok_spec(d),
        out_shape=jax.ShapeDtypeStruct((bsz, s, d), F32),
        compiler_params=pltpu.CompilerParams(
            dimension_semantics=("parallel", "arbitrary"), vmem_limit_bytes=VMEM_LIMIT_BYTES,
            allow_input_fusion=[False] * 7 + [w.dtype == BF16 for w in wts]),
        name="post",
    )(x, ya, yb, mods, mods, mods, mods, wo, l1g, l1b, w1, b1, w2, b2, l2g, l2b)


def _proj_weights(w_in, w_s, b_s, ln_v_g, ln_v_b, conv_qk, b_gates, hn_g):
    q0 = 2 * D_GMLP
    v0 = q0 + 2 * D_QK
    o0 = v0 + D_MLSTM
    g0 = o0 + D_MLSTM
    def gate_tiles(a):
        return a.reshape(2, 2, H_B, -1).transpose(2, 1, 0, 3).reshape(H_B * N_GATE, -1)

    wuv = w_in[:, :q0].astype(BF16)
    wqk = w_in[:, q0:v0].astype(BF16)
    wvt = jnp.concatenate([w_in[:, v0:o0].T, gate_tiles(w_in[:, g0:].T)], axis=0).astype(BF16)
    wo = w_in[:, o0:g0].astype(BF16)
    ws = w_s.astype(BF16)
    bs = jnp.broadcast_to(b_s[:, :, None], (G_A, CHUNK, CHUNK))
    qs = jnp.concatenate([jnp.full((D_QK,), HK ** -0.5, F32), jnp.ones((D_QK,), F32)])[None, :]
    return (wuv, wqk, wvt, wo, ws, bs, ln_v_g[None, :], ln_v_b[None, :], conv_qk, qs, gate_tiles(b_gates[:, None]),
            hn_g[None, :])


def kernel(x, c, ctx, c_ctx, w_ada, b_ada, w_in, w_s, b_s, ln_v_g, ln_v_b, conv_qk, b_gates, hn_g,
           w_out, ln1_g, ln1_b, w1, b1, w2, b2, ln2_g, ln2_b):
    bsz = x.shape[0]
    l = 0
    cond = jnp.concatenate([c, c_ctx[None, :], jnp.zeros((ADA_ROWS - bsz - 1, D_MODEL), F32)], axis=0)
    mods = _ada(cond, w_ada[l], b_ada[l][None, :])[:, None, :]

    pw = _proj_weights(w_in[l], w_s[l], b_s[l], ln_v_g[l], ln_v_b[l], conv_qk[l], b_gates[l], hn_g[l])
    ya, kk, qt, vt, og, gr = _proj(x, mods, None, pw, PROJ_TILE)
    _, kkc, _, vtc, _, grc = _proj(ctx, mods, bsz, pw, ctx.shape[1])

    gq, gqc, ct = _gates(gr, grc)
    yb = _mlstm(kk, qt, vt, og, gq, ct, kkc, vtc, gqc)

    row = lambda v: v[None, :]
    post_w = (w_out[l].astype(BF16), row(ln1_g[l]), row(ln1_b[l]), w1[l].astype(BF16), row(b1[l]),
              w2[l].astype(BF16), row(b2[l]), row(ln2_g[l]), row(ln2_b[l]))
    return _post(x, ya, yb, mods, post_w, POST_TILE)
```
